```python
import math
import jax, jax.numpy as jnp
from jax import lax
import numpy as np

D_MODEL = 1024
BATCH = 4
SEQ = 4096
DEPTH = 2
DEC_BATCH = 128
DEC_SEQ = 8
PAST_LEN = 2048
PAGE_SIZE = 128

N_MIXERS = 2
N_SB = (DEPTH + 1) // 2
N_DIFF = DEPTH // 2
SB_HEADS = 16
SB_HEAD_DIM = D_MODEL // SB_HEADS
DIFF_HEADS = 8
DIFF_HEAD_DIM = D_MODEL // (2 * DIFF_HEADS)
ROPE_DIM = DIFF_HEAD_DIM // 4
ROPE_THETA = 500000.0
Q_BLOCK = 128
N_KEYS = 128
N_EXPERTS = N_KEYS * N_KEYS
PEER_HEADS = 8
PEER_TOPK = 16
PEER_QUERY_DIM = 256
PEER_HALF = PEER_QUERY_DIM // 2
PEER_BLOCK = 256
RMS_EPS = 1e-6

kernel_name = "stickbreak_diffattn_peer_adaln_decode_step"


def rms_norm(x, gain):
    xf = x.astype(jnp.float32)
    y = xf * lax.rsqrt(jnp.mean(xf * xf, axis=-1, keepdims=True) + RMS_EPS)
    return (y * gain.astype(jnp.float32)).astype(x.dtype)


def rope_partial(x, pos):
    half = ROPE_DIM // 2
    inv = ROPE_THETA ** (-jnp.arange(half, dtype=jnp.float32) / half)
    ang = pos.astype(jnp.float32)[:, None] * inv[None, :]
    ang = ang.reshape((1, pos.shape[0]) + (1,) * (x.ndim - 3) + (half,))
    cos, sin = jnp.cos(ang), jnp.sin(ang)
    xr = x[..., :ROPE_DIM].astype(jnp.float32)
    x1, x2 = xr[..., :half], xr[..., half:]
    rot = jnp.concatenate([x1 * cos - x2 * sin, x2 * cos + x1 * sin], axis=-1)
    return jnp.concatenate([rot.astype(x.dtype), x[..., ROPE_DIM:]], axis=-1)


def sweep_query_blocks(fn, q, q_pos):
    B, Tq = q.shape[0], q.shape[1]
    if Tq > Q_BLOCK and Tq % Q_BLOCK == 0:
        nb = Tq // Q_BLOCK
        qb = jnp.moveaxis(q.reshape((B, nb, Q_BLOCK) + q.shape[2:]), 1, 0)
        pb = q_pos.reshape(nb, Q_BLOCK)
        out = lax.map(lambda a: fn(a[0], a[1]), (qb, pb))
        return jnp.moveaxis(out, 0, 1).reshape((B, Tq) + out.shape[3:])
    return fn(q, q_pos)


def stick_breaking_mixer(h, pos, k_past, v_past, w_qkv, w_o):
    B, T, _ = h.shape
    qkv = (h @ w_qkv).reshape(B, T, 3, SB_HEADS, SB_HEAD_DIM)
    q, k, v = qkv[:, :, 0], qkv[:, :, 1], qkv[:, :, 2]
    k_all = k if k_past is None else jnp.concatenate([k_past, k], axis=1)
    v_all = v if v_past is None else jnp.concatenate([v_past, v], axis=1)
    k_pos = jnp.arange(k_all.shape[1])
    scale = SB_HEAD_DIM ** -0.5

    def block(qb, pb):
        z = jnp.einsum('bqhd,bkhd->bhqk', qb, k_all).astype(jnp.float32) * scale
        causal = k_pos[None, :] < pb[:, None]
        log_beta = jax.nn.log_sigmoid(z)
        log_keep = jnp.where(causal, jax.nn.log_sigmoid(-z), 0.0)
        after = lax.cumsum(log_keep, axis=3, reverse=True) - log_keep
        w = jnp.where(causal, jnp.exp(log_beta + after), 0.0)
        return jnp.einsum('bhqk,bkhd->bqhd', w.astype(v_all.dtype), v_all)

    o = sweep_query_blocks(block, q, pos)
    return o.reshape(B, T, D_MODEL) @ w_o, k, v


def diff_mixer(h, pos, k_past, v_past, w_qkv, w_o, q_gain, k_gain,
               lq1, lk1, lq2, lk2, sub_gain, lambda_init):
    B, T, _ = h.shape
    qkv = h @ w_qkv
    q = qkv[..., :D_MODEL].reshape(B, T, DIFF_HEADS, 2, DIFF_HEAD_DIM)
    k = qkv[..., D_MODEL:2 * D_MODEL].reshape(B, T, DIFF_HEADS, 2, DIFF_HEAD_DIM)
    v = qkv[..., 2 * D_MODEL:].reshape(B, T, DIFF_HEADS, 2 * DIFF_HEAD_DIM)
    q = rope_partial(rms_norm(q, q_gain), pos)
    k = rope_partial(rms_norm(k, k_gain), pos)
    f32 = jnp.float32
    lam = (jnp.exp(jnp.sum(lq1.astype(f32) * lk1.astype(f32)))
           - jnp.exp(jnp.sum(lq2.astype(f32) * lk2.astype(f32))) + lambda_init)
    k_all = k if k_past is None else jnp.concatenate([k_past, k], axis=1)
    v_all = v if v_past is None else jnp.concatenate([v_past, v], axis=1)
    k_pos = jnp.arange(k_all.shape[1])
    scale = DIFF_HEAD_DIM ** -0.5

    def block(qb, pb):
        s = jnp.einsum('bqhcd,bkhcd->bhcqk', qb, k_all).astype(f32) * scale
        causal = k_pos[None, :] <= pb[:, None]
        p = jax.nn.softmax(jnp.where(causal, s, -jnp.inf), axis=-1)
        a = p[:, :, 0] - lam * p[:, :, 1]
        return jnp.einsum('bhqk,bkhe->bqhe', a.astype(v_all.dtype), v_all)

    o = sweep_query_blocks(block, q, pos)
    o = rms_norm(o, sub_gain) * (1.0 - lambda_init)
    return o.reshape(B, T, D_MODEL) @ w_o, k, v


def peer_ffn(h, w_query, sub_keys, expert_u, expert_v):
    B, T, D = h.shape
    n = B * T
    x = jnp.pad(h.reshape(n, D), ((0, (-n) % PEER_BLOCK), (0, 0)))
    xb = x.reshape(-1, PEER_BLOCK, D)

    def block(xt):
        q = (xt @ w_query).reshape(PEER_BLOCK, PEER_HEADS, 2, PEER_HALF)
        s = jnp.einsum('thcd,cnd->thcn', q, sub_keys).astype(jnp.float32)
        s1, i1 = lax.top_k(s[:, :, 0], PEER_TOPK)
        s2, i2 = lax.top_k(s[:, :, 1], PEER_TOPK)
        n_cand = PEER_TOPK * PEER_TOPK
        cand = (s1[..., :, None] + s2[..., None, :]).reshape(PEER_BLOCK, PEER_HEADS, n_cand)
        cidx = (i1[..., :, None] * N_KEYS + i2[..., None, :]).reshape(PEER_BLOCK, PEER_HEADS, n_cand)
        top_s, top_p = lax.top_k(cand, PEER_TOPK)
        idx = jnp.take_along_axis(cidx, top_p, axis=-1)
        g = jax.nn.softmax(top_s, axis=-1)
        act = jax.nn.gelu(jnp.einsum('thkd,td->thk', expert_u[idx], xt))
        coef = (g * act.astype(jnp.float32)).astype(xt.dtype)
        return jnp.einsum('thk,thkd->td', coef, expert_v[idx])

    y = lax.map(block, xb).reshape(-1, D)[:n]
    return y.reshape(B, T, D)


def modulation(c, w_ada, b_ada):
    m = jax.nn.silu(c) @ w_ada + b_ada
    return tuple(t[:, None, :] for t in jnp.split(m, 6, axis=-1))


def run_trunk(x, c, pos, cache_k_sb, cache_v_sb, cache_k_diff, cache_v_diff, page_table,
              w_ada, b_ada, norm_mix, norm_ffn, w_qkv_sb, w_o_sb, w_qkv_diff, w_o_diff,
              diff_q_gain, diff_k_gain, diff_lambda_q1, diff_lambda_k1, diff_lambda_q2,
              diff_lambda_k2, diff_sub_gain, peer_w_query, peer_sub_keys, peer_u, peer_v):
    B = x.shape[0]

    def gather(pool):
        if pool is None:
            return None
        g = pool[page_table]
        return g.reshape((B, -1) + pool.shape[2:])

    sb_k, sb_v, df_k, df_v = [], [], [], []
    for i in range(DEPTH):
        sh_m, sc_m, g_m, sh_f, sc_f, g_f = modulation(c, w_ada[i], b_ada[i])
        h = rms_norm(x, norm_mix[i]) * (1.0 + sc_m) + sh_m
        j = i // N_MIXERS
        if i % N_MIXERS == 0:
            kp = gather(None if cache_k_sb is None else cache_k_sb[j])
            vp = gather(None if cache_v_sb is None else cache_v_sb[j])
            out, k, v = stick_breaking_mixer(h, pos, kp, vp, w_qkv_sb[j], w_o_sb[j])
            sb_k.append(k)
            sb_v.append(v)
        else:
            lambda_init = 0.8 - 0.6 * math.exp(-0.3 * i)
            kp = gather(None if cache_k_diff is None else cache_k_diff[j])
            vp = gather(None if cache_v_diff is None else cache_v_diff[j])
            out, k, v = diff_mixer(h, pos, kp, vp, w_qkv_diff[j], w_o_diff[j],
                                   diff_q_gain[j], diff_k_gain[j], diff_lambda_q1[j],
                                   diff_lambda_k1[j], diff_lambda_q2[j], diff_lambda_k2[j],
                                   diff_sub_gain[j], lambda_init)
            df_k.append(k)
            df_v.append(v)
        x = x + g_m * out
        h = rms_norm(x, norm_ffn[i]) * (1.0 + sc_f) + sh_f
        x = x + g_f * peer_ffn(h, peer_w_query[i], peer_sub_keys[i], peer_u[i], peer_v[i])
    return x, jnp.stack(sb_k), jnp.stack(sb_v), jnp.stack(df_k), jnp.stack(df_v)


def setup_inputs(seed: int = 0) -> dict:
    key = jax.random.key(seed)
    ks = jax.random.split(key, 32)
    f32 = jnp.float32
    n_pages = PAST_LEN // PAGE_SIZE
    n_used = DEC_BATCH * n_pages
    n_phys = n_used + (n_used + 3) // 4

    def nrm(k, shape, s=1.0):
        return jax.random.normal(k, shape, f32) * s

    inv_d = D_MODEL ** -0.5
    perm = jax.random.permutation(ks[8], n_phys)
    page_table = perm[:n_used].reshape(DEC_BATCH, n_pages).astype(jnp.int32)
    return {
        "x_prompt": nrm(ks[0], (BATCH, SEQ, D_MODEL)),
        "x_sample": nrm(ks[1], (DEC_BATCH, DEC_SEQ, D_MODEL)),
        "cache_k_sb": nrm(ks[4], (N_SB, n_phys, PAGE_SIZE, SB_HEADS, SB_HEAD_DIM)),
        "cache_v_sb": nrm(ks[5], (N_SB, n_phys, PAGE_SIZE, SB_HEADS, SB_HEAD_DIM)),
        "cache_k_diff": nrm(ks[6], (N_DIFF, n_phys, PAGE_SIZE, DIFF_HEADS, 2, DIFF_HEAD_DIM)),
        "cache_v_diff": nrm(ks[7], (N_DIFF, n_phys, PAGE_SIZE, DIFF_HEADS, 2 * DIFF_HEAD_DIM)),
        "page_table": page_table,
        "c_prompt": nrm(ks[2], (BATCH, D_MODEL)),
        "c_sample": nrm(ks[3], (DEC_BATCH, D_MODEL)),
        "w_ada": nrm(ks[9], (DEPTH, D_MODEL, 6 * D_MODEL), inv_d),
        "b_ada": nrm(ks[10], (DEPTH, 6 * D_MODEL), 0.02),
        "norm_mix": 1.0 + nrm(ks[11], (DEPTH, D_MODEL), 0.02),
        "norm_ffn": 1.0 + nrm(ks[12], (DEPTH, D_MODEL), 0.02),
        "w_qkv_sb": nrm(ks[13], (N_SB, D_MODEL, 3 * D_MODEL), inv_d),
        "w_o_sb": nrm(ks[14], (N_SB, D_MODEL, D_MODEL), inv_d),
        "w_qkv_diff": nrm(ks[15], (N_DIFF, D_MODEL, 3 * D_MODEL), inv_d),
        "w_o_diff": nrm(ks[16], (N_DIFF, D_MODEL, D_MODEL), inv_d),
        "diff_q_gain": 1.0 + nrm(ks[17], (N_DIFF, DIFF_HEAD_DIM), 0.02),
        "diff_k_gain": 1.0 + nrm(ks[18], (N_DIFF, DIFF_HEAD_DIM), 0.02),
        "diff_lambda_q1": nrm(ks[19], (N_DIFF, DIFF_HEAD_DIM), 0.1),
        "diff_lambda_k1": nrm(ks[20], (N_DIFF, DIFF_HEAD_DIM), 0.1),
        "diff_lambda_q2": nrm(ks[21], (N_DIFF, DIFF_HEAD_DIM), 0.1),
        "diff_lambda_k2": nrm(ks[22], (N_DIFF, DIFF_HEAD_DIM), 0.1),
        "diff_sub_gain": 1.0 + nrm(ks[23], (N_DIFF, 2 * DIFF_HEAD_DIM), 0.02),
        "peer_w_query": nrm(ks[24], (DEPTH, D_MODEL, PEER_HEADS * PEER_QUERY_DIM), inv_d),
        "peer_sub_keys": nrm(ks[25], (DEPTH, 2, N_KEYS, PEER_HALF), PEER_HALF ** -0.5),
        "peer_u": nrm(ks[26], (DEPTH, N_EXPERTS, D_MODEL), inv_d),
        "peer_v": nrm(ks[27], (DEPTH, N_EXPERTS, D_MODEL), PEER_HEADS ** -0.5),
    }


def reference(x_prompt, x_sample, cache_k_sb, cache_v_sb, cache_k_diff, cache_v_diff,
              page_table, c_prompt, c_sample, w_ada, b_ada, norm_mix, norm_ffn,
              w_qkv_sb, w_o_sb, w_qkv_diff, w_o_diff, diff_q_gain, diff_k_gain,
              diff_lambda_q1, diff_lambda_k1, diff_lambda_q2, diff_lambda_k2,
              diff_sub_gain, peer_w_query, peer_sub_keys, peer_u, peer_v):
    past_len = page_table.shape[1] * cache_k_sb.shape[2]
    pos_prompt = jnp.arange(x_prompt.shape[1])
    pos_sample = past_len + jnp.arange(x_sample.shape[1])
    y_prompt, k_sb_p, v_sb_p, k_df_p, v_df_p = run_trunk(
        x_prompt, c_prompt, pos_prompt, None, None, None, None, None,
        w_ada, b_ada, norm_mix, norm_ffn, w_qkv_sb, w_o_sb, w_qkv_diff, w_o_diff,
        diff_q_gain, diff_k_gain, diff_lambda_q1, diff_lambda_k1, diff_lambda_q2,
        diff_lambda_k2, diff_sub_gain, peer_w_query, peer_sub_keys, peer_u, peer_v)
    y_sample, k_sb_s, v_sb_s, k_df_s, v_df_s = run_trunk(
        x_sample, c_sample, pos_sample, cache_k_sb, cache_v_sb, cache_k_diff, cache_v_diff,
        page_table,
        w_ada, b_ada, norm_mix, norm_ffn, w_qkv_sb, w_o_sb, w_qkv_diff, w_o_diff,
        diff_q_gain, diff_k_gain, diff_lambda_q1, diff_lambda_k1, diff_lambda_q2,
        diff_lambda_k2, diff_sub_gain, peer_w_query, peer_sub_keys, peer_u, peer_v)
    return (y_prompt, y_sample, k_sb_p, v_sb_p, k_df_p, v_df_p, k_sb_s, v_sb_s, k_df_s, v_df_s)
```

```python
import functools
import math

import jax
import jax.numpy as jnp
from jax import lax
from jax.experimental import pallas as pl
from jax.experimental.pallas import tpu as pltpu

F32 = jnp.float32
BF16 = jnp.bfloat16

RMS_EPS = 1e-6
ROPE_THETA = 500000.0
SB_HEADS = 16
DIFF_HEADS = 8
HEAD_DIM = 64
ROPE_DIM = HEAD_DIM // 4
N_KEYS = 128
PEER_HEADS = 8
PEER_TOPK = 16
LANES = 128
SUBLANES = 8
VMEM_LIMIT = 56 * 1024 * 1024
EXP_ZERO_BELOW = -105.0


def _cparams(*sem):
    return pltpu.CompilerParams(dimension_semantics=sem, vmem_limit_bytes=VMEM_LIMIT)


def _pick(n, choices):
    for c in choices:
        if n % c == 0:
            return c
    raise ValueError(f"no block size in {choices} divides {n}")


def _dot(a, b):
    return jnp.dot(a, b, preferred_element_type=F32)


def _dot_nt(a, b):
    return lax.dot_general(a, b, (((1,), (1,)), ((), ())), preferred_element_type=F32)


def _split3(a):
    hi = a.astype(BF16)
    r = a - hi.astype(F32)
    mid = r.astype(BF16)
    lo = (r - mid.astype(F32)).astype(BF16)
    return hi, mid, lo


def _dot_f32_lhs(a, m):
    hi, mid, lo = _split3(a)
    return _dot(hi, m) + _dot(mid, m) + _dot(lo, m)


def _dot_f32_rhs(m, a):
    hi, mid, lo = _split3(a)
    return _dot(m, hi) + _dot(m, mid) + _dot(m, lo)


def _softplus(z):
    return jnp.maximum(z, 0.0) + jnp.log(1.0 + jnp.exp(-jnp.abs(z)))


def _norm_mod(x, gain, sc, sh):
    ms = jnp.mean(x * x, axis=-1, keepdims=True)
    y = x * lax.rsqrt(ms + RMS_EPS)
    return (y * gain) * (1.0 + sc) + sh


def _ada_kernel(c_ref, w_ref, b_ref, o_ref):
    c = c_ref[...]
    s = c / (1.0 + jnp.exp(-c))
    o_ref[0] = _dot(s.astype(BF16), w_ref[0].astype(BF16)) + b_ref[0]


def _ada(c_all, w_ada, b_ada):
    n_layers, d, n = w_ada.shape
    r = c_all.shape[0]
    tn = d
    return pl.pallas_call(
        _ada_kernel,
        grid=(n_layers, n // tn),
        in_specs=[
            pl.BlockSpec((r, d), lambda l, j: (0, 0)),
            pl.BlockSpec((1, d, tn), lambda l, j: (l, 0, j)),
            pl.BlockSpec((1, 1, tn), lambda l, j: (l, 0, j)),
        ],
        out_specs=pl.BlockSpec((1, r, tn), lambda l, j: (l, 0, j)),
        out_shape=jax.ShapeDtypeStruct((n_layers, r, n), F32),
        compiler_params=_cparams("arbitrary", "arbitrary"),
        name="ada_modulation",
    )(c_all, w_ada, b_ada.reshape(n_layers, 1, n))


def _qkv_kernel(x_ref, g_ref, sc_ref, sh_ref, w_ref, q_ref, k_ref, v_ref):
    d = x_ref.shape[1]
    h = _norm_mod(x_ref[...], g_ref[...], sc_ref[...], sh_ref[...]).astype(BF16)
    r = _dot(h, w_ref[...])
    q_ref[...] = r[:, :d]
    k_ref[...] = r[:, d:2 * d]
    v_ref[...] = r[:, 2 * d:]


def _qkv(x, gain, sc, sh, w_bf16):
    t, d = x.shape
    tm = _pick(t, (256, 128))
    row = pl.BlockSpec((tm, d), lambda i: (i, 0))
    return pl.pallas_call(
        _qkv_kernel,
        grid=(t // tm,),
        in_specs=[row, pl.BlockSpec((1, d), lambda i: (0, 0)), row, row,
                  pl.BlockSpec((d, 3 * d), lambda i: (0, 0))],
        out_specs=[row, row, row],
        out_shape=[jax.ShapeDtypeStruct((t, d), F32)] * 3,
        compiler_params=_cparams("arbitrary"),
        name="norm_qkv",
    )(x, gain.reshape(1, d), sc, sh, w_bf16)


def _qknorm_rope_kernel(q_ref, k_ref, qg_ref, kg_ref, c_ref, s1_ref, s2_ref, qo_ref, ko_ref):
    d = q_ref.shape[1]
    r = lax.broadcasted_iota(jnp.int32, (LANES, LANES), 0) // HEAD_DIM
    c = lax.broadcasted_iota(jnp.int32, (LANES, LANES), 1) // HEAD_DIM
    seg = jnp.where(r == c, 1.0, 0.0).astype(BF16)
    cosv, s1, s2 = c_ref[...], s1_ref[...], s2_ref[...]
    for src, gain, dst in ((q_ref, qg_ref, qo_ref), (k_ref, kg_ref, ko_ref)):
        g = gain[...]
        for j in range(d // LANES):
            x = src[:, j * LANES:(j + 1) * LANES]
            ms = _dot_f32_lhs(x * x, seg) * (1.0 / HEAD_DIM)
            y = x * lax.rsqrt(ms + RMS_EPS) * g
            up = pltpu.roll(y, LANES - ROPE_DIM // 2, 1)
            dn = pltpu.roll(y, ROPE_DIM // 2, 1)
            dst[:, j * LANES:(j + 1) * LANES] = y * cosv + up * s1 + dn * s2


def _qknorm_rope(q, k, q_gain, k_gain, cos_t, s1_t, s2_t):
    t, d = q.shape
    tm = _pick(t, (512, 256, 128))
    row = pl.BlockSpec((tm, d), lambda i: (i, 0))
    tab = pl.BlockSpec((tm, LANES), lambda i: (i, 0))
    gs = pl.BlockSpec((1, LANES), lambda i: (0, 0))
    tile = LANES // HEAD_DIM
    return pl.pallas_call(
        _qknorm_rope_kernel,
        grid=(t // tm,),
        in_specs=[row, row, gs, gs, tab, tab, tab],
        out_specs=[row, row],
        out_shape=[jax.ShapeDtypeStruct((t, d), F32)] * 2,
        compiler_params=_cparams("arbitrary"),
        name="qknorm_rope",
    )(q, k, jnp.tile(q_gain, tile).reshape(1, LANES), jnp.tile(k_gain, tile).reshape(1, LANES),
      cos_t, s1_t, s2_t)


def _rope_tables(pos):
    half = ROPE_DIM // 2
    inv = ROPE_THETA ** (-jnp.arange(half, dtype=F32) / half)
    ang = pos.astype(F32)[:, None] * inv[None, :]
    cos, sin = jnp.cos(ang), jnp.sin(ang)
    t = pos.shape[0]
    pad = jnp.zeros((t, HEAD_DIM - ROPE_DIM), F32)
    z = jnp.zeros((t, half), F32)
    c = jnp.concatenate([cos, cos, pad + 1.0], axis=1)
    s1 = jnp.concatenate([-sin, z, pad], axis=1)
    s2 = jnp.concatenate([z, sin, pad], axis=1)
    rep = LANES // HEAD_DIM
    return jnp.tile(c, (1, rep)), jnp.tile(s1, (1, rep)), jnp.tile(s2, (1, rep))


def _sb_prompt_kernel(q_ref, k_ref, v_ref, o_ref, *, tq, tk, scale):
    i = pl.program_id(2)
    q0 = i * tq
    jr = lax.broadcasted_iota(jnp.int32, (tk, tk), 0)
    jc = lax.broadcasted_iota(jnp.int32, (tk, tk), 1)
    later = jnp.where(jr > jc, 1.0, 0.0).astype(BF16)
    row = q0 + lax.broadcasted_iota(jnp.int32, (tq, 1), 0)
    n_kb = (q0 + tq) // tk
    outs = []
    for hh in range(LANES // HEAD_DIM):
        lo, hi = hh * HEAD_DIM, (hh + 1) * HEAD_DIM
        q = (q_ref[:, lo:hi] * scale).astype(BF16)

        def cond(carry):
            kb, _, cs = carry
            return jnp.logical_and(kb >= 0, jnp.max(cs) > EXP_ZERO_BELOW)

        def body(carry):
            kb, acc, cs = carry
            k0 = pl.multiple_of(kb * tk, tk)
            kblk = k_ref[pl.ds(k0, tk), lo:hi].astype(BF16)
            vblk = v_ref[pl.ds(k0, tk), lo:hi].astype(BF16)
            z = _dot_nt(q, kblk)
            col = k0 + lax.broadcasted_iota(jnp.int32, (1, tk), 1)
            causal = col < row
            sp = _softplus(z)
            log_keep = jnp.where(causal, -sp, 0.0)
            after = _dot_f32_lhs(log_keep, later) + cs
            w = jnp.where(causal, jnp.exp((z - sp) + after), 0.0)
            acc = acc + _dot(w.astype(BF16), vblk)
            cs = cs + jnp.sum(log_keep, axis=1, keepdims=True)
            return kb - 1, acc, cs

        _, acc, _ = lax.while_loop(
            cond, body, (n_kb - 1, jnp.zeros((tq, HEAD_DIM), F32), jnp.zeros((tq, 1), F32)))
        outs.append(acc)
    o_ref[...] = jnp.concatenate(outs, axis=1)


def _sb_prompt(q, k, v, batch, seq):
    d = q.shape[1]
    tq = _pick(seq, (256, 128))
    tk = 128
    nq = seq // tq
    return pl.pallas_call(
        functools.partial(_sb_prompt_kernel, tq=tq, tk=tk, scale=HEAD_DIM ** -0.5),
        grid=(batch, d // LANES, nq),
        in_specs=[
            pl.BlockSpec((tq, LANES), lambda b, h, i: (b * nq + i, h)),
            pl.BlockSpec((seq, LANES), lambda b, h, i: (b, h)),
            pl.BlockSpec((seq, LANES), lambda b, h, i: (b, h)),
        ],
        out_specs=pl.BlockSpec((tq, LANES), lambda b, h, i: (b * nq + i, h)),
        out_shape=jax.ShapeDtypeStruct((batch * seq, d), F32),
        compiler_params=_cparams("arbitrary", "arbitrary", "arbitrary"),
        name="sb_prompt_attention",
    )(q, k, v)


def _lambda(lq1_ref, lk1_ref, lq2_ref, lk2_ref, lambda_init):
    a = jnp.sum(lq1_ref[...] * lk1_ref[...], axis=1, keepdims=True)
    b = jnp.sum(lq2_ref[...] * lk2_ref[...], axis=1, keepdims=True)
    return jnp.exp(a) - jnp.exp(b) + lambda_init


def _diff_prompt_kernel(q_ref, k_ref, v_ref, lq1_ref, lk1_ref, lq2_ref, lk2_ref, sg_ref, o_ref,
                        *, tq, scale, lambda_init):
    i = pl.program_id(2)
    q0 = i * tq
    tk = tq
    lam = _lambda(lq1_ref, lk1_ref, lq2_ref, lk2_ref, lambda_init)
    qs = [(q_ref[:, c * HEAD_DIM:(c + 1) * HEAD_DIM] * scale).astype(BF16) for c in range(2)]
    row = lax.broadcasted_iota(jnp.int32, (tq, 1), 0)
    col = lax.broadcasted_iota(jnp.int32, (1, tk), 1)
    diag_mask = col <= row

    def step(kb, carry, masked):
        k0 = pl.multiple_of(kb * tk, tk)
        vblk = v_ref[pl.ds(k0, tk), :].astype(BF16)
        new = []
        for c in range(2):
            m, l, acc = carry[c]
            kblk = k_ref[pl.ds(k0, tk), c * HEAD_DIM:(c + 1) * HEAD_DIM].astype(BF16)
            s = _dot_nt(qs[c], kblk)
            if masked:
                s = jnp.where(diag_mask, s, -jnp.inf)
            m_new = jnp.maximum(m, jnp.max(s, axis=1, keepdims=True))
            alpha = jnp.exp(m - m_new)
            p = jnp.exp(s - m_new)
            l = l * alpha + jnp.sum(p, axis=1, keepdims=True)
            acc = acc * alpha + _dot(p.astype(BF16), vblk)
            new.append((m_new, l, acc))
        return tuple(new)

    init = tuple((jnp.full((tq, 1), -jnp.inf, F32), jnp.zeros((tq, 1), F32),
                  jnp.zeros((tq, LANES), F32)) for _ in range(2))
    carry = lax.fori_loop(0, i, lambda kb, cr: step(kb, cr, False), init)
    (_, l0, a0), (_, l1, a1) = step(i, carry, True)
    o = a0 / l0 - lam * (a1 / l1)
    ms = jnp.mean(o * o, axis=-1, keepdims=True)
    o_ref[...] = (o * lax.rsqrt(ms + RMS_EPS) * sg_ref[...]) * (1.0 - lambda_init)


def _diff_prompt(q, k, v, lam_vecs, sub_gain, lambda_init, batch, seq):
    d = q.shape[1]
    tq = _pick(seq, (256, 128))
    nq = seq // tq
    vec = pl.BlockSpec((1, HEAD_DIM), lambda b, h, i: (0, 0))
    return pl.pallas_call(
        functools.partial(_diff_prompt_kernel, tq=tq, scale=HEAD_DIM ** -0.5,
                          lambda_init=lambda_init),
        grid=(batch, d // LANES, nq),
        in_specs=[
            pl.BlockSpec((tq, LANES), lambda b, h, i: (b * nq + i, h)),
            pl.BlockSpec((seq, LANES), lambda b, h, i: (b, h)),
            pl.BlockSpec((seq, LANES), lambda b, h, i: (b, h)),
            vec, vec, vec, vec,
            pl.BlockSpec((1, LANES), lambda b, h, i: (0, 0)),
        ],
        out_specs=pl.BlockSpec((tq, LANES), lambda b, h, i: (b * nq + i, h)),
        out_shape=jax.ShapeDtypeStruct((batch * seq, d), F32),
        compiler_params=_cparams("arbitrary", "arbitrary", "arbitrary"),
        name="diff_prompt_attention",
    )(q, k, v, *[x.reshape(1, HEAD_DIM) for x in lam_vecs], sub_gain.reshape(1, LANES))


def _block_diag_queries(q, scale):
    n_q, d = q.shape
    groups = d // HEAD_DIM
    rep = jnp.concatenate([q * scale] * groups, axis=0)
    rg = lax.broadcasted_iota(jnp.int32, rep.shape, 0) // n_q
    cg = lax.broadcasted_iota(jnp.int32, rep.shape, 1) // HEAD_DIM
    return jnp.where(rg == cg, rep, 0.0).astype(BF16)


def _pad_rows(x, rows):
    return jnp.concatenate([x, jnp.zeros((rows - x.shape[0], x.shape[1]), x.dtype)], axis=0)


def _col_form(row_vec):
    return jnp.transpose(jnp.broadcast_to(row_vec, (LANES, LANES)))


def _scale_rows(acc, row_vec):
    m = _col_form(row_vec)
    return jnp.concatenate([acc[:, j * LANES:(j + 1) * LANES] * m
                            for j in range(acc.shape[1] // LANES)], axis=1)


def _take_group_blocks(acc, n_q, width):
    rows, d = acc.shape
    a3 = acc.reshape(rows // n_q, n_q, d)
    g = lax.broadcasted_iota(jnp.int32, a3.shape, 0)
    cg = lax.broadcasted_iota(jnp.int32, a3.shape, 2) // width
    return jnp.sum(jnp.where(g == cg, a3, 0.0), axis=0)


def _sb_decode_kernel(pt_ref, q_ref, kn_ref, vn_ref, kp_ref, vp_ref, o_ref,
                      qbd_sc, acc_sc, cs_sc, *, scale):
    s = pl.program_id(1)
    n_q = q_ref.shape[0]
    page = kp_ref.shape[0]

    def page_step(kblk, vblk, valid):
        z = _dot_nt(kblk.astype(BF16), qbd_sc[...])
        sp = _softplus(z)
        log_keep = -sp if valid is None else jnp.where(valid, -sp, 0.0)
        jr = lax.broadcasted_iota(jnp.int32, (page, page), 0)
        jc = lax.broadcasted_iota(jnp.int32, (page, page), 1)
        later = jnp.where(jc > jr, 1.0, 0.0).astype(BF16)
        after = _dot_f32_rhs(later, log_keep) + cs_sc[...]
        w = jnp.exp((z - sp) + after)
        if valid is not None:
            w = jnp.where(valid, w, 0.0)
        acc_sc[...] += _dot(jnp.transpose(w).astype(BF16), vblk.astype(BF16))
        cs_sc[...] += jnp.sum(log_keep, axis=0, keepdims=True)

    @pl.when(s == 0)
    def _():
        qbd_sc[...] = _block_diag_queries(q_ref[...], scale)
        acc_sc[...] = jnp.zeros_like(acc_sc)
        cs_sc[...] = jnp.zeros_like(cs_sc)
        key = lax.broadcasted_iota(jnp.int32, (page, LANES), 0)
        qi = lax.broadcasted_iota(jnp.int32, (page, LANES), 1) % n_q
        page_step(_pad_rows(kn_ref[...], page), _pad_rows(vn_ref[...], page), key < qi)

    @pl.when(s > 0)
    def _():
        page_step(kp_ref[...], vp_ref[...], None)

    @pl.when(s == pl.num_programs(1) - 1)
    def _():
        o_ref[...] = _take_group_blocks(acc_sc[...], n_q, HEAD_DIM)


def _diff_decode_kernel(pt_ref, q_ref, kn_ref, vn_ref, kp_ref, vp_ref,
                        lq1_ref, lk1_ref, lq2_ref, lk2_ref, sg_ref, o_ref,
                        qbd_sc, acc_sc, m_sc, l_sc, *, scale, lambda_init):
    s = pl.program_id(1)
    n_q = q_ref.shape[0]
    page = kp_ref.shape[0]

    def page_step(kblk, vblk, valid):
        z = _dot_nt(kblk.astype(BF16), qbd_sc[...])
        if valid is not None:
            z = jnp.where(valid, z, -jnp.inf)
        m_old = m_sc[...]
        m_new = jnp.maximum(m_old, jnp.max(z, axis=0, keepdims=True))
        alpha = jnp.exp(m_old - m_new)
        p = jnp.exp(z - m_new)
        l_sc[...] = l_sc[...] * alpha + jnp.sum(p, axis=0, keepdims=True)
        m_sc[...] = m_new
        acc_sc[...] = (_scale_rows(acc_sc[...], alpha)
                       + _dot(jnp.transpose(p).astype(BF16), vblk.astype(BF16)))

    @pl.when(s == 0)
    def _():
        qbd_sc[...] = _block_diag_queries(q_ref[...], scale)
        acc_sc[...] = jnp.zeros_like(acc_sc)
        m_sc[...] = jnp.full_like(m_sc, -jnp.inf)
        l_sc[...] = jnp.zeros_like(l_sc)
        key = lax.broadcasted_iota(jnp.int32, (page, LANES), 0)
        qi = lax.broadcasted_iota(jnp.int32, (page, LANES), 1) % n_q
        page_step(_pad_rows(kn_ref[...], page), _pad_rows(vn_ref[...], page), key <= qi)

    @pl.when(s > 0)
    def _():
        page_step(kp_ref[...], vp_ref[...], None)

    @pl.when(s == pl.num_programs(1) - 1)
    def _():
        lam = _lambda(lq1_ref, lk1_ref, lq2_ref, lk2_ref, lambda_init)
        d = acc_sc.shape[1]
        full = _scale_rows(acc_sc[...], 1.0 / l_sc[...])
        a4 = full.reshape(DIFF_HEADS, 2, n_q, d)
        hd = lax.broadcasted_iota(jnp.int32, (DIFF_HEADS, n_q, d), 0)
        cg = lax.broadcasted_iota(jnp.int32, (DIFF_HEADS, n_q, d), 2) // LANES
        o0 = jnp.sum(jnp.where(hd == cg, a4[:, 0], 0.0), axis=0)
        o1 = jnp.sum(jnp.where(hd == cg, a4[:, 1], 0.0), axis=0)
        o = o0 - lam * o1
        sg = sg_ref[...]
        for j in range(d // LANES):
            blk = o[:, j * LANES:(j + 1) * LANES]
            ms = jnp.mean(blk * blk, axis=-1, keepdims=True)
            o_ref[:, j * LANES:(j + 1) * LANES] = (
                (blk * lax.rsqrt(ms + RMS_EPS) * sg) * (1.0 - lambda_init))


def _decode_attention(kind, q, k, v, cache_k, cache_v, page_table, row0, extra=(), **kw):
    d = q.shape[1]
    n_seq, n_pages = page_table.shape
    n_phys, page = cache_k.shape[0], cache_k.shape[1]
    ck = cache_k.reshape(n_phys, page, d)
    cv = cache_v.reshape(n_phys, page, d)
    n_q = SUBLANES
    blk0 = row0 // n_q
    new = pl.BlockSpec((n_q, d), lambda b, s, pt: (blk0 + b, 0))
    pg = pl.BlockSpec((None, page, d),
                      lambda b, s, pt: (pt[b, n_pages - jnp.maximum(s, 1)], 0, 0))
    if kind == "sb":
        body = functools.partial(_sb_decode_kernel, **kw)
        extra_specs = []
        scratch = [pltpu.VMEM((LANES, d), BF16), pltpu.VMEM((LANES, d), F32),
                   pltpu.VMEM((1, LANES), F32)]
    else:
        body = functools.partial(_diff_decode_kernel, **kw)
        vec = pl.BlockSpec((1, HEAD_DIM), lambda b, s, pt: (0, 0))
        extra_specs = [vec, vec, vec, vec, pl.BlockSpec((1, LANES), lambda b, s, pt: (0, 0))]
        scratch = [pltpu.VMEM((LANES, d), BF16), pltpu.VMEM((LANES, d), F32),
                   pltpu.VMEM((1, LANES), F32), pltpu.VMEM((1, LANES), F32)]
    return pl.pallas_call(
        body,
        grid_spec=pltpu.PrefetchScalarGridSpec(
            num_scalar_prefetch=1,
            grid=(n_seq, n_pages + 1),
            in_specs=[new, new, new, pg, pg] + extra_specs,
            out_specs=pl.BlockSpec((n_q, d), lambda b, s, pt: (b, 0)),
            scratch_shapes=scratch,
        ),
        out_shape=jax.ShapeDtypeStruct((n_seq * n_q, d), F32),
        compiler_params=_cparams("arbitrary", "arbitrary"),
        name=kind + "_decode_attention",
    )(page_table, q, k, v, ck, cv, *extra)


def _outproj_kernel(o_ref, w_ref, x_ref, g_ref, y_ref):
    y_ref[...] = x_ref[...] + g_ref[...] * _dot(o_ref[...].astype(BF16), w_ref[...])


def _outproj(o, w_bf16, x, gate):
    t, d = x.shape
    tm = _pick(t, (512, 256, 128))
    row = pl.BlockSpec((tm, d), lambda i: (i, 0))
    return pl.pallas_call(
        _outproj_kernel,
        grid=(t // tm,),
        in_specs=[row, pl.BlockSpec((d, d), lambda i: (0, 0)), row, row],
        out_specs=row,
        out_shape=jax.ShapeDtypeStruct((t, d), F32),
        compiler_params=_cparams("arbitrary"),
        name="outproj_residual",
    )(o, w_bf16, x, gate)


def _peer_scores_kernel(x_ref, g_ref, sc_ref, sh_ref, wqt_ref, keys_ref, h_ref, s_ref):
    h = _norm_mod(x_ref[...], g_ref[...], sc_ref[...], sh_ref[...]).astype(BF16)
    h_ref[...] = h
    qt = _dot_nt(wqt_ref[...], h)
    n_hc = s_ref.shape[0]
    half = keys_ref.shape[2]
    for hc in range(n_hc):
        s_ref[hc] = _dot(keys_ref[hc % 2], qt[hc * half:(hc + 1) * half].astype(BF16))


def _peer_scores(x, gain, sc, sh, wqt_bf16, keys_bf16):
    t, d = x.shape
    nq = wqt_bf16.shape[0]
    half = keys_bf16.shape[2]
    n_hc = nq // half
    tm = _pick(t, (256, 128))
    row = pl.BlockSpec((tm, d), lambda i: (i, 0))
    return pl.pallas_call(
        _peer_scores_kernel,
        grid=(t // tm,),
        in_specs=[row, pl.BlockSpec((1, d), lambda i: (0, 0)), row, row,
                  pl.BlockSpec((nq, d), lambda i: (0, 0)),
                  pl.BlockSpec(keys_bf16.shape, lambda i: (0, 0, 0))],
        out_specs=[row, pl.BlockSpec((n_hc, N_KEYS, tm), lambda i: (0, 0, i))],
        out_shape=[jax.ShapeDtypeStruct((t, d), BF16),
                   jax.ShapeDtypeStruct((n_hc, N_KEYS, t), F32)],
        compiler_params=_cparams("arbitrary"),
        name="peer_scores",
    )(x, gain.reshape(1, d), sc, sh, wqt_bf16, keys_bf16)


def _extract_topk(vals, k):
    n = vals.shape[0]
    idx = lax.broadcasted_iota(jnp.int32, vals.shape, 0)
    rank = jnp.full(vals.shape, float(k), F32)
    tops, where_ = [], []
    for r in range(k):
        m = jnp.max(vals, axis=0, keepdims=True)
        first = jnp.min(jnp.where(vals == m, idx, n), axis=0, keepdims=True)
        hit = idx == first
        vals = jnp.where(hit, -jnp.inf, vals)
        rank = jnp.where(hit, float(r), rank)
        tops.append(m)
        where_.append(first)
    return tops, where_, rank


def _peer_topk_kernel(s_ref, r2_ref, w2_ref, lr_ref, w1_ref):
    k = PEER_TOPK
    s1, s2 = s_ref[0], s_ref[1]
    t1, _, rank1 = _extract_topk(s1, k)
    t2, _, rank2 = _extract_topk(s2, k)
    top2 = jnp.concatenate(t2, axis=0)
    cand = jnp.concatenate([t1[r] + top2 for r in range(k)], axis=0)
    tops, pos, _ = _extract_topk(cand, k)
    r1_iota = lax.broadcasted_iota(jnp.int32, (k, cand.shape[1]), 0)
    count = jnp.zeros((k, cand.shape[1]), F32)
    z = jnp.zeros((1, cand.shape[1]), F32)
    for j in range(k):
        count = count + jnp.where(r1_iota == pos[j] // k, 1.0, 0.0)
        z = z + jnp.exp(tops[j] - tops[0])
    lrow = jnp.zeros(s1.shape, F32)
    for r in range(k):
        lrow = jnp.where(rank1 == float(r), count[r:r + 1], lrow)
    r2_ref[0] = rank2
    lr_ref[0] = lrow
    w1_ref[0] = jnp.where(rank1 < float(k), jnp.exp(s1 - t1[0]), 0.0)
    w2_ref[0] = jnp.where(rank2 < float(k), jnp.exp(s2 - t2[0]), 0.0) / z


def _peer_topk(s):
    n_hc, n_keys, t = s.shape
    heads = n_hc // 2
    tb = LANES
    out = pl.BlockSpec((1, n_keys, tb), lambda i, h: (h, 0, i))
    return pl.pallas_call(
        _peer_topk_kernel,
        grid=(t // tb, heads),
        in_specs=[pl.BlockSpec((2, n_keys, tb), lambda i, h: (h, 0, i))],
        out_specs=[out] * 4,
        out_shape=[jax.ShapeDtypeStruct((heads, n_keys, t), F32)] * 4,
        compiler_params=_cparams("arbitrary", "arbitrary"),
        name="peer_topk",
    )(s)


def _gelu_tanh(a):
    return 0.5 * a * (1.0 + jnp.tanh(math.sqrt(2.0 / math.pi) * (a + 0.044715 * (a * a * a))))


def _peer_expert_kernel(h_ref, r2_ref, w2_ref, lr_ref, w1_ref, u_ref, vt_ref, x_ref, g_ref,
                        o_ref, acc_sc, coef_sc):
    e = pl.program_id(1)
    heads, n_keys, _ = r2_ref.shape
    rows_per_step = u_ref.shape[0] // n_keys

    @pl.when(e == 0)
    def _():
        acc_sc[...] = jnp.zeros_like(acc_sc)

    act_t = _dot_nt(u_ref[...], h_ref[...])
    for kk in range(rows_per_step):
        i1 = e * rows_per_step + kk
        cmat = None
        for hd in range(heads):
            keep = r2_ref[hd] < lr_ref[hd, pl.ds(i1, 1), :]
            term = jnp.where(keep, w2_ref[hd], 0.0) * w1_ref[hd, pl.ds(i1, 1), :]
            cmat = term if cmat is None else cmat + term
        a = act_t[kk * n_keys:(kk + 1) * n_keys]
        coef_sc[kk * n_keys:(kk + 1) * n_keys, :] = (cmat * _gelu_tanh(a)).astype(BF16)
    acc_sc[...] += _dot(vt_ref[...], coef_sc[...])

    @pl.when(e == pl.num_programs(1) - 1)
    def _():
        o_ref[...] = x_ref[...] + g_ref[...] * jnp.transpose(acc_sc[...])


def _peer_expert(h, r2, w2, lr, w1, u_bf16, vt_bf16, x, gate):
    t, d = x.shape
    heads, n_keys, _ = r2.shape
    n_exp = u_bf16.shape[0]
    tb = _pick(t, (512, 256, 128))
    eb = 512
    row = pl.BlockSpec((tb, d), lambda i, e: (i, 0))
    sel = pl.BlockSpec((heads, n_keys, tb), lambda i, e: (0, 0, i))
    return pl.pallas_call(
        _peer_expert_kernel,
        grid=(t // tb, n_exp // eb),
        in_specs=[row, sel, sel, sel, sel,
                  pl.BlockSpec((eb, d), lambda i, e: (e, 0)),
                  pl.BlockSpec((d, eb), lambda i, e: (0, e)),
                  row, row],
        out_specs=row,
        out_shape=jax.ShapeDtypeStruct((t, d), F32),
        scratch_shapes=[pltpu.VMEM((d, tb), F32), pltpu.VMEM((eb, tb), BF16)],
        compiler_params=_cparams("arbitrary", "arbitrary"),
        name="peer_experts",
    )(h, r2, w2, lr, w1, u_bf16, vt_bf16, x, gate)


def _peer(x, gain, sc, sh, gate, w_query, sub_keys, expert_u, expert_v):
    h, s = _peer_scores(x, gain, sc, sh, jnp.transpose(w_query).astype(BF16),
                        sub_keys.astype(BF16))
    r2, w2, lr, w1 = _peer_topk(s)
    return _peer_expert(h, r2, w2, lr, w1, expert_u.astype(BF16),
                        jnp.transpose(expert_v).astype(BF16), x, gate)


def kernel(x_prompt, x_sample, cache_k_sb, cache_v_sb, cache_k_diff, cache_v_diff, page_table, c_prompt, c_sample, w_ada, b_ada, norm_mix, norm_ffn, w_qkv_sb, w_o_sb, w_qkv_diff, w_o_diff, diff_q_gain, diff_k_gain, diff_lambda_q1, diff_lambda_k1, diff_lambda_q2, diff_lambda_k2, diff_sub_gain, peer_w_query, peer_sub_keys, peer_u, peer_v):
    batch, seq, d = x_prompt.shape
    n_seq, n_new, _ = x_sample.shape
    assert d == SB_HEADS * HEAD_DIM == DIFF_HEADS * 2 * HEAD_DIM and n_new == SUBLANES
    depth = w_ada.shape[0]
    tp, ts = batch * seq, n_seq * n_new
    past_len = page_table.shape[1] * cache_k_sb.shape[2]

    x = jnp.concatenate([x_prompt.reshape(tp, d), x_sample.reshape(ts, d)], axis=0)
    c_all = jnp.concatenate([c_prompt, c_sample], axis=0)
    n_c = c_all.shape[0]
    c_all = jnp.pad(c_all, ((0, (-n_c) % SUBLANES), (0, 0)))
    mod = _ada(c_all, w_ada, b_ada)

    def per_token(layer, j):
        m = mod[layer, :, j * d:(j + 1) * d]
        return jnp.concatenate([jnp.repeat(m[:batch], seq, axis=0),
                                jnp.repeat(m[batch:n_c], n_new, axis=0)], axis=0)

    pos = jnp.concatenate([jnp.tile(jnp.arange(seq), batch),
                           jnp.tile(past_len + jnp.arange(n_new), n_seq)])
    cos_t, s1_t, s2_t = _rope_tables(pos)

    sb_k, sb_v, df_k, df_v = [], [], [], []
    for i in range(depth):
        sh_m, sc_m, g_m, sh_f, sc_f, g_f = [per_token(i, j) for j in range(6)]
        j = i // 2
        if i % 2 == 0:
            q, k, v = _qkv(x, norm_mix[i], sc_m, sh_m, w_qkv_sb[j].astype(BF16))
            o_p = _sb_prompt(q, k, v, batch, seq)
            o_s = _decode_attention("sb", q, k, v, cache_k_sb[j], cache_v_sb[j], page_table, tp,
                                    scale=HEAD_DIM ** -0.5)
            w_o = w_o_sb[j]
            sb_k.append(k)
            sb_v.append(v)
        else:
            lambda_init = 0.8 - 0.6 * math.exp(-0.3 * i)
            q, k, v = _qkv(x, norm_mix[i], sc_m, sh_m, w_qkv_diff[j].astype(BF16))
            q, k = _qknorm_rope(q, k, diff_q_gain[j], diff_k_gain[j], cos_t, s1_t, s2_t)
            lam_vecs = (diff_lambda_q1[j], diff_lambda_k1[j], diff_lambda_q2[j], diff_lambda_k2[j])
            o_p = _diff_prompt(q, k, v, lam_vecs, diff_sub_gain[j], lambda_init, batch, seq)
            o_s = _decode_attention(
                "diff", q, k, v, cache_k_diff[j], cache_v_diff[j], page_table, tp,
                extra=tuple(a.reshape(1, HEAD_DIM) for a in lam_vecs)
                + (diff_sub_gain[j].reshape(1, LANES),),
                scale=HEAD_DIM ** -0.5, lambda_init=lambda_init)
            w_o = w_o_diff[j]
            df_k.append(k)
            df_v.append(v)
        x = _outproj(jnp.concatenate([o_p, o_s], axis=0), w_o.astype(BF16), x, g_m)
        x = _peer(x, norm_ffn[i], sc_f, sh_f, g_f, peer_w_query[i], peer_sub_keys[i],
                  peer_u[i], peer_v[i])

    def rows(stack, lo, hi, shape):
        return jnp.stack([a[lo:hi].reshape(shape) for a in stack])

    t = tp + ts
    return (
        x[:tp].reshape(batch, seq, d),
        x[tp:].reshape(n_seq, n_new, d),
        rows(sb_k, 0, tp, (batch, seq, SB_HEADS, HEAD_DIM)),
        rows(sb_v, 0, tp, (batch, seq, SB_HEADS, HEAD_DIM)),
        rows(df_k, 0, tp, (batch, seq, DIFF_HEADS, 2, HEAD_DIM)),
        rows(df_v, 0, tp, (batch, seq, DIFF_HEADS, 2 * HEAD_DIM)),
        rows(sb_k, tp, t, (n_seq, n_new, SB_HEADS, HEAD_DIM)),
        rows(sb_v, tp, t, (n_seq, n_new, SB_HEADS, HEAD_DIM)),
        rows(df_k, tp, t, (n_seq, n_new, DIFF_HEADS, 2, HEAD_DIM)),
        rows(df_v, tp, t, (n_seq, n_new, DIFF_HEADS, 2 * HEAD_DIM)),
    )
```

```python
import functools
import math

import jax
import jax.numpy as jnp
from jax import lax
from jax.experimental import pallas as pl
from jax.experimental.pallas import tpu as pltpu

F32 = jnp.float32
BF16 = jnp.bfloat16

RMS_EPS = 1e-6
ROPE_THETA = 500000.0
SB_HEADS = 16
DIFF_HEADS = 8
HEAD_DIM = 64
ROPE_DIM = HEAD_DIM // 4
N_KEYS = 128
PEER_HEADS = 8
PEER_TOPK = 16
LANES = 128
SUBLANES = 8
VMEM_LIMIT = 56 * 1024 * 1024
EXP_ZERO_BELOW = -105.0


def _cparams(*sem):
    return pltpu.CompilerParams(dimension_semantics=sem, vmem_limit_bytes=VMEM_LIMIT)


def _pick(n, choices):
    for c in choices:
        if n % c == 0:
            return c
    raise ValueError(f"no block size in {choices} divides {n}")


def _dot(a, b):
    return jnp.dot(a, b, preferred_element_type=F32)


def _dot_nt(a, b):
    return lax.dot_general(a, b, (((1,), (1,)), ((), ())), preferred_element_type=F32)


def _split3(a):
    hi = a.astype(BF16)
    r = a - hi.astype(F32)
    mid = r.astype(BF16)
    lo = (r - mid.astype(F32)).astype(BF16)
    return hi, mid, lo


def _dot_f32_lhs(a, m):
    hi, mid, lo = _split3(a)
    return _dot(hi, m) + _dot(mid, m) + _dot(lo, m)


def _dot_f32_rhs(m, a):
    hi, mid, lo = _split3(a)
    return _dot(m, hi) + _dot(m, mid) + _dot(m, lo)


def _softplus(z):
    return jnp.maximum(z, 0.0) + jnp.log(1.0 + jnp.exp(-jnp.abs(z)))


def _norm_mod(x, gain, sc, sh):
    ms = jnp.mean(x * x, axis=-1, keepdims=True)
    y = x * lax.rsqrt(ms + RMS_EPS)
    return (y * gain) * (1.0 + sc) + sh


def _ada_kernel(c_ref, w_ref, b_ref, o_ref):
    c = c_ref[...]
    s = c / (1.0 + jnp.exp(-c))
    o_ref[0] = _dot(s.astype(BF16), w_ref[0].astype(BF16)) + b_ref[0]


def _ada(c_all, w_ada, b_ada):
    n_layers, d, n = w_ada.shape
    r = c_all.shape[0]
    tn = d
    return pl.pallas_call(
        _ada_kernel,
        grid=(n_layers, n // tn),
        in_specs=[
            pl.BlockSpec((r, d), lambda l, j: (0, 0)),
            pl.BlockSpec((1, d, tn), lambda l, j: (l, 0, j)),
            pl.BlockSpec((1, 1, tn), lambda l, j: (l, 0, j)),
        ],
        out_specs=pl.BlockSpec((1, r, tn), lambda l, j: (l, 0, j)),
        out_shape=jax.ShapeDtypeStruct((n_layers, r, n), F32),
        compiler_params=_cparams("arbitrary", "arbitrary"),
        name="ada_modulation",
    )(c_all, w_ada, b_ada.reshape(n_layers, 1, n))


def _qkv_kernel(x_ref, g_ref, sc_ref, sh_ref, w_ref, q_ref, k_ref, v_ref):
    d = x_ref.shape[1]
    h = _norm_mod(x_ref[...], g_ref[...], sc_ref[...], sh_ref[...]).astype(BF16)
    r = _dot(h, w_ref[...])
    q_ref[...] = r[:, :d]
    k_ref[...] = r[:, d:2 * d]
    v_ref[...] = r[:, 2 * d:]


def _qkv(x, gain, sc, sh, w_bf16):
    t, d = x.shape
    tm = _pick(t, (256, 128))
    row = pl.BlockSpec((tm, d), lambda i: (i, 0))
    return pl.pallas_call(
        _qkv_kernel,
        grid=(t // tm,),
        in_specs=[row, pl.BlockSpec((1, d), lambda i: (0, 0)), row, row,
                  pl.BlockSpec((d, 3 * d), lambda i: (0, 0))],
        out_specs=[row, row, row],
        out_shape=[jax.ShapeDtypeStruct((t, d), F32)] * 3,
        compiler_params=_cparams("arbitrary"),
        name="norm_qkv",
    )(x, gain.reshape(1, d), sc, sh, w_bf16)


def _qknorm_rope_kernel(q_ref, k_ref, qg_ref, kg_ref, c_ref, s1_ref, s2_ref, qo_ref, ko_ref):
    d = q_ref.shape[1]
    r = lax.broadcasted_iota(jnp.int32, (LANES, LANES), 0) // HEAD_DIM
    c = lax.broadcasted_iota(jnp.int32, (LANES, LANES), 1) // HEAD_DIM
    seg = jnp.where(r == c, 1.0, 0.0).astype(BF16)
    cosv, s1, s2 = c_ref[...], s1_ref[...], s2_ref[...]
    for src, gain, dst in ((q_ref, qg_ref, qo_ref), (k_ref, kg_ref, ko_ref)):
        g = gain[...]
        for j in range(d // LANES):
            x = src[:, j * LANES:(j + 1) * LANES]
            ms = _dot_f32_lhs(x * x, seg) * (1.0 / HEAD_DIM)
            y = x * lax.rsqrt(ms + RMS_EPS) * g
            up = pltpu.roll(y, LANES - ROPE_DIM // 2, 1)
            dn = pltpu.roll(y, ROPE_DIM // 2, 1)
            dst[:, j * LANES:(j + 1) * LANES] = y * cosv + up * s1 + dn * s2


def _qknorm_rope(q, k, q_gain, k_gain, cos_t, s1_t, s2_t):
    t, d = q.shape
    tm = _pick(t, (512, 256, 128))
    row = pl.BlockSpec((tm, d), lambda i: (i, 0))
    tab = pl.BlockSpec((tm, LANES), lambda i: (i, 0))
    gs = pl.BlockSpec((1, LANES), lambda i: (0, 0))
    tile = LANES // HEAD_DIM
    return pl.pallas_call(
        _qknorm_rope_kernel,
        grid=(t // tm,),
        in_specs=[row, row, gs, gs, tab, tab, tab],
        out_specs=[row, row],
        out_shape=[jax.ShapeDtypeStruct((t, d), F32)] * 2,
        compiler_params=_cparams("arbitrary"),
        name="qknorm_rope",
    )(q, k, jnp.tile(q_gain, tile).reshape(1, LANES), jnp.tile(k_gain, tile).reshape(1, LANES),
      cos_t, s1_t, s2_t)


def _rope_tables(pos):
    half = ROPE_DIM // 2
    inv = ROPE_THETA ** (-jnp.arange(half, dtype=F32) / half)
    ang = pos.astype(F32)[:, None] * inv[None, :]
    cos, sin = jnp.cos(ang), jnp.sin(ang)
    t = pos.shape[0]
    pad = jnp.zeros((t, HEAD_DIM - ROPE_DIM), F32)
    z = jnp.zeros((t, half), F32)
    c = jnp.concatenate([cos, cos, pad + 1.0], axis=1)
    s1 = jnp.concatenate([-sin, z, pad], axis=1)
    s2 = jnp.concatenate([z, sin, pad], axis=1)
    rep = LANES // HEAD_DIM
    return jnp.tile(c, (1, rep)), jnp.tile(s1, (1, rep)), jnp.tile(s2, (1, rep))


def _sb_prompt_kernel(q_ref, k_ref, v_ref, o_ref, *, tq, tk, scale):
    i = pl.program_id(2)
    q0 = i * tq
    jr = lax.broadcasted_iota(jnp.int32, (tk, tk), 0)
    jc = lax.broadcasted_iota(jnp.int32, (tk, tk), 1)
    later = jnp.where(jr > jc, 1.0, 0.0).astype(BF16)
    row = q0 + lax.broadcasted_iota(jnp.int32, (tq, 1), 0)
    n_kb = (q0 + tq) // tk
    n_heads = LANES // HEAD_DIM
    qs = [(q_ref[:, hh * HEAD_DIM:(hh + 1) * HEAD_DIM] * scale).astype(BF16)
          for hh in range(n_heads)]

    def cond(carry):
        kb, state = carry
        live = functools.reduce(jnp.maximum, [jnp.max(cs) for _, cs in state])
        return jnp.logical_and(kb >= 0, live > EXP_ZERO_BELOW)

    def body(carry):
        kb, state = carry
        k0 = pl.multiple_of(kb * tk, tk)
        kblk = k_ref[pl.ds(k0, tk), :].astype(BF16)
        vblk = v_ref[pl.ds(k0, tk), :].astype(BF16)
        col = k0 + lax.broadcasted_iota(jnp.int32, (1, tk), 1)
        causal = col < row
        new = []
        for hh, (acc, cs) in enumerate(state):
            lo, hi = hh * HEAD_DIM, (hh + 1) * HEAD_DIM
            z = _dot_nt(qs[hh], kblk[:, lo:hi])
            sp = _softplus(z)
            log_keep = jnp.where(causal, -sp, 0.0)
            after = _dot_f32_lhs(log_keep, later) + cs
            w = jnp.where(causal, jnp.exp((z - sp) + after), 0.0)
            acc = acc + _dot(w.astype(BF16), vblk[:, lo:hi])
            cs = cs + jnp.sum(log_keep, axis=1, keepdims=True)
            new.append((acc, cs))
        return kb - 1, tuple(new)

    init = tuple((jnp.zeros((tq, HEAD_DIM), F32), jnp.zeros((tq, 1), F32))
                 for _ in range(n_heads))
    _, state = lax.while_loop(cond, body, (n_kb - 1, init))
    o_ref[...] = jnp.concatenate([acc for acc, _ in state], axis=1)


def _sb_prompt(q, k, v, batch, seq):
    d = q.shape[1]
    tq = _pick(seq, (256, 128))
    tk = 128
    nq = seq // tq
    return pl.pallas_call(
        functools.partial(_sb_prompt_kernel, tq=tq, tk=tk, scale=HEAD_DIM ** -0.5),
        grid=(batch, d // LANES, nq),
        in_specs=[
            pl.BlockSpec((tq, LANES), lambda b, h, i: (b * nq + i, h)),
            pl.BlockSpec((seq, LANES), lambda b, h, i: (b, h)),
            pl.BlockSpec((seq, LANES), lambda b, h, i: (b, h)),
        ],
        out_specs=pl.BlockSpec((tq, LANES), lambda b, h, i: (b * nq + i, h)),
        out_shape=jax.ShapeDtypeStruct((batch * seq, d), F32),
        compiler_params=_cparams("arbitrary", "arbitrary", "arbitrary"),
        name="sb_prompt_attention",
    )(q, k, v)


def _lambda(lq1_ref, lk1_ref, lq2_ref, lk2_ref, lambda_init):
    a = jnp.sum(lq1_ref[...] * lk1_ref[...], axis=1, keepdims=True)
    b = jnp.sum(lq2_ref[...] * lk2_ref[...], axis=1, keepdims=True)
    return jnp.exp(a) - jnp.exp(b) + lambda_init


def _diff_prompt_kernel(q_ref, k_ref, v_ref, lq1_ref, lk1_ref, lq2_ref, lk2_ref, sg_ref, o_ref,
                        *, tq, scale, lambda_init):
    i = pl.program_id(2)
    q0 = i * tq
    tk = tq
    lam = _lambda(lq1_ref, lk1_ref, lq2_ref, lk2_ref, lambda_init)
    qs = [(q_ref[:, c * HEAD_DIM:(c + 1) * HEAD_DIM] * scale).astype(BF16) for c in range(2)]
    row = lax.broadcasted_iota(jnp.int32, (tq, 1), 0)
    col = lax.broadcasted_iota(jnp.int32, (1, tk), 1)
    diag_mask = col <= row

    def step(kb, carry, masked):
        k0 = pl.multiple_of(kb * tk, tk)
        vblk = v_ref[pl.ds(k0, tk), :].astype(BF16)
        vext = jnp.concatenate([vblk, jnp.ones_like(vblk)], axis=1)
        new = []
        for c in range(2):
            m, acc = carry[c]
            kblk = k_ref[pl.ds(k0, tk), c * HEAD_DIM:(c + 1) * HEAD_DIM].astype(BF16)
            s = _dot_nt(qs[c], kblk)
            if masked:
                s = jnp.where(diag_mask, s, -jnp.inf)
            m_new = jnp.maximum(m, jnp.max(s, axis=1, keepdims=True))
            p = jnp.exp(s - m_new)
            acc = acc * jnp.exp(m - m_new) + _dot(p.astype(BF16), vext)
            new.append((m_new, acc))
        return tuple(new)

    init = tuple((jnp.full((tq, 1), -jnp.inf, F32), jnp.zeros((tq, 2 * LANES), F32))
                 for _ in range(2))
    carry = lax.fori_loop(0, i, lambda kb, cr: step(kb, cr, False), init)
    (_, a0), (_, a1) = step(i, carry, True)
    o = a0[:, :LANES] / a0[:, LANES:] - lam * (a1[:, :LANES] / a1[:, LANES:])
    ms = jnp.mean(o * o, axis=-1, keepdims=True)
    o_ref[...] = (o * lax.rsqrt(ms + RMS_EPS) * sg_ref[...]) * (1.0 - lambda_init)


def _diff_prompt(q, k, v, lam_vecs, sub_gain, lambda_init, batch, seq):
    d = q.shape[1]
    tq = _pick(seq, (512, 256, 128))
    nq = seq // tq
    vec = pl.BlockSpec((1, HEAD_DIM), lambda b, h, i: (0, 0))
    return pl.pallas_call(
        functools.partial(_diff_prompt_kernel, tq=tq, scale=HEAD_DIM ** -0.5,
                          lambda_init=lambda_init),
        grid=(batch, d // LANES, nq),
        in_specs=[
            pl.BlockSpec((tq, LANES), lambda b, h, i: (b * nq + i, h)),
            pl.BlockSpec((seq, LANES), lambda b, h, i: (b, h)),
            pl.BlockSpec((seq, LANES), lambda b, h, i: (b, h)),
            vec, vec, vec, vec,
            pl.BlockSpec((1, LANES), lambda b, h, i: (0, 0)),
        ],
        out_specs=pl.BlockSpec((tq, LANES), lambda b, h, i: (b * nq + i, h)),
        out_shape=jax.ShapeDtypeStruct((batch * seq, d), F32),
        compiler_params=_cparams("arbitrary", "arbitrary", "arbitrary"),
        name="diff_prompt_attention",
    )(q, k, v, *[x.reshape(1, HEAD_DIM) for x in lam_vecs], sub_gain.reshape(1, LANES))


def _block_diag_queries(q, scale):
    n_q, d = q.shape
    groups = d // HEAD_DIM
    rep = jnp.concatenate([q * scale] * groups, axis=0)
    rg = lax.broadcasted_iota(jnp.int32, rep.shape, 0) // n_q
    cg = lax.broadcasted_iota(jnp.int32, rep.shape, 1) // HEAD_DIM
    return jnp.where(rg == cg, rep, 0.0).astype(BF16)


def _pad_rows(x, rows):
    return jnp.concatenate([x, jnp.zeros((rows - x.shape[0], x.shape[1]), x.dtype)], axis=0)


def _col_form(row_vec):
    return jnp.transpose(jnp.broadcast_to(row_vec, (LANES, LANES)))


def _scale_rows(acc, row_vec):
    m = _col_form(row_vec)
    return jnp.concatenate([acc[:, j * LANES:(j + 1) * LANES] * m
                            for j in range(acc.shape[1] // LANES)], axis=1)


def _take_group_blocks(acc, n_q, width):
    rows, d = acc.shape
    a3 = acc.reshape(rows // n_q, n_q, d)
    g = lax.broadcasted_iota(jnp.int32, a3.shape, 0)
    cg = lax.broadcasted_iota(jnp.int32, a3.shape, 2) // width
    return jnp.sum(jnp.where(g == cg, a3, 0.0), axis=0)


def _new_row_mask(page, n_q, strict):
    key = lax.broadcasted_iota(jnp.int32, (page, LANES), 0)
    qi = lax.broadcasted_iota(jnp.int32, (page, LANES), 1) % n_q
    return key < qi if strict else key <= qi


def _sb_blocks(blocks, qbd, cs):
    page = blocks[0][0].shape[0]
    jr = lax.broadcasted_iota(jnp.int32, (page, page), 0)
    jc = lax.broadcasted_iota(jnp.int32, (page, page), 1)
    later = jnp.where(jc > jr, 1.0, 0.0).astype(BF16)
    zs = [_dot_nt(kblk.astype(BF16), qbd) for kblk, _, _ in blocks]
    out = None
    for (_, vblk, valid), z in zip(blocks, zs):
        sp = _softplus(z)
        log_keep = -sp if valid is None else jnp.where(valid, -sp, 0.0)
        after = _dot_f32_rhs(later, log_keep) + cs
        w = jnp.exp((z - sp) + after)
        if valid is not None:
            w = jnp.where(valid, w, 0.0)
        c = _dot(jnp.transpose(w).astype(BF16), vblk.astype(BF16))
        out = c if out is None else out + c
        cs = cs + jnp.sum(log_keep, axis=0, keepdims=True)
    return out, cs


def _sb_decode_kernel(pt_ref, q_ref, kn_ref, vn_ref, *rest, scale, pps):
    kp, vp, o_ref = rest[:pps], rest[pps:2 * pps], rest[2 * pps]
    qbd_sc, acc_sc, cs_sc = rest[2 * pps + 1:]
    s = pl.program_id(1)
    n_q = q_ref.shape[0]
    page = kp[0].shape[0]

    @pl.when(s == 0)
    def _():
        qbd = _block_diag_queries(q_ref[...], scale)
        qbd_sc[...] = qbd
        new = (_pad_rows(kn_ref[...], page), _pad_rows(vn_ref[...], page),
               _new_row_mask(page, n_q, True))
        acc_sc[...], cs_sc[...] = _sb_blocks([new], qbd, jnp.zeros(cs_sc.shape, F32))

    @pl.when(jnp.max(cs_sc[...]) > EXP_ZERO_BELOW)
    def _():
        out, cs = _sb_blocks([(kp[i][...], vp[i][...], None) for i in range(pps)],
                             qbd_sc[...], cs_sc[...])
        acc_sc[...] += out
        cs_sc[...] = cs

    @pl.when(s == pl.num_programs(1) - 1)
    def _():
        o_ref[...] = _take_group_blocks(acc_sc[...], n_q, HEAD_DIM)


def _diff_blocks(blocks, qbd, m_old, l_old, acc_old):
    zs = []
    for kblk, _, valid in blocks:
        z = _dot_nt(kblk.astype(BF16), qbd)
        zs.append(z if valid is None else jnp.where(valid, z, -jnp.inf))
    m_new = functools.reduce(jnp.maximum, [jnp.max(z, axis=0, keepdims=True) for z in zs])
    if acc_old is None:
        l, acc = jnp.zeros_like(m_new), None
    else:
        m_new = jnp.maximum(m_old, m_new)
        alpha = jnp.exp(m_old - m_new)
        l, acc = l_old * alpha, _scale_rows(acc_old, alpha)
    for (_, vblk, _), z in zip(blocks, zs):
        p = jnp.exp(z - m_new)
        l = l + jnp.sum(p, axis=0, keepdims=True)
        c = _dot(jnp.transpose(p).astype(BF16), vblk.astype(BF16))
        acc = c if acc is None else acc + c
    return m_new, l, acc


def _diff_decode_kernel(pt_ref, q_ref, kn_ref, vn_ref, *rest, scale, lambda_init, pps):
    kp, vp = rest[:pps], rest[pps:2 * pps]
    lq1_ref, lk1_ref, lq2_ref, lk2_ref, sg_ref, o_ref = rest[2 * pps:2 * pps + 6]
    qbd_sc, acc_sc, m_sc, l_sc = rest[2 * pps + 6:]
    s = pl.program_id(1)
    n_q = q_ref.shape[0]
    page = kp[0].shape[0]

    @pl.when(s == 0)
    def _():
        qbd = _block_diag_queries(q_ref[...], scale)
        qbd_sc[...] = qbd
        new = (_pad_rows(kn_ref[...], page), _pad_rows(vn_ref[...], page),
               _new_row_mask(page, n_q, False))
        m_sc[...], l_sc[...], acc_sc[...] = _diff_blocks([new], qbd, None, None, None)

    m_sc[...], l_sc[...], acc_sc[...] = _diff_blocks(
        [(kp[i][...], vp[i][...], None) for i in range(pps)],
        qbd_sc[...], m_sc[...], l_sc[...], acc_sc[...])

    @pl.when(s == pl.num_programs(1) - 1)
    def _():
        lam = _lambda(lq1_ref, lk1_ref, lq2_ref, lk2_ref, lambda_init)
        d = acc_sc.shape[1]
        full = _scale_rows(acc_sc[...], 1.0 / l_sc[...])
        a4 = full.reshape(DIFF_HEADS, 2, n_q, d)
        hd = lax.broadcasted_iota(jnp.int32, (DIFF_HEADS, n_q, d), 0)
        cg = lax.broadcasted_iota(jnp.int32, (DIFF_HEADS, n_q, d), 2) // LANES
        o0 = jnp.sum(jnp.where(hd == cg, a4[:, 0], 0.0), axis=0)
        o1 = jnp.sum(jnp.where(hd == cg, a4[:, 1], 0.0), axis=0)
        o = o0 - lam * o1
        sg = sg_ref[...]
        for j in range(d // LANES):
            blk = o[:, j * LANES:(j + 1) * LANES]
            ms = jnp.mean(blk * blk, axis=-1, keepdims=True)
            o_ref[:, j * LANES:(j + 1) * LANES] = (
                (blk * lax.rsqrt(ms + RMS_EPS) * sg) * (1.0 - lambda_init))


def _decode_attention(kind, q, k, v, cache_k, cache_v, layer, page_table, row0, extra=(), **kw):
    d = q.shape[1]
    n_seq, n_pages = page_table.shape
    n_layers, n_phys, page = cache_k.shape[:3]
    ck = cache_k.reshape(n_layers, n_phys, page, d)
    cv = cache_v.reshape(n_layers, n_phys, page, d)
    n_q = SUBLANES
    blk0 = row0 // n_q
    pps = _pick(n_pages, (4, 2, 1))
    new = pl.BlockSpec((n_q, d), lambda b, s, pt: (blk0 + b, 0))
    pages = [pl.BlockSpec((None, None, page, d),
                          lambda b, s, pt, i=i: (layer, pt[b, n_pages - 1 - (s * pps + i)], 0, 0))
             for i in range(pps)]
    if kind == "sb":
        body = functools.partial(_sb_decode_kernel, pps=pps, **kw)
        extra_specs = []
        scratch = [pltpu.VMEM((LANES, d), BF16), pltpu.VMEM((LANES, d), F32),
                   pltpu.VMEM((1, LANES), F32)]
    else:
        body = functools.partial(_diff_decode_kernel, pps=pps, **kw)
        vec = pl.BlockSpec((1, HEAD_DIM), lambda b, s, pt: (0, 0))
        extra_specs = [vec, vec, vec, vec, pl.BlockSpec((1, LANES), lambda b, s, pt: (0, 0))]
        scratch = [pltpu.VMEM((LANES, d), BF16), pltpu.VMEM((LANES, d), F32),
                   pltpu.VMEM((1, LANES), F32), pltpu.VMEM((1, LANES), F32)]
    return pl.pallas_call(
        body,
        grid_spec=pltpu.PrefetchScalarGridSpec(
            num_scalar_prefetch=1,
            grid=(n_seq, n_pages // pps),
            in_specs=[new, new, new] + pages + pages + extra_specs,
            out_specs=pl.BlockSpec((n_q, d), lambda b, s, pt: (b, 0)),
            scratch_shapes=scratch,
        ),
        out_shape=jax.ShapeDtypeStruct((n_seq * n_q, d), F32),
        compiler_params=_cparams("arbitrary", "arbitrary"),
        name=kind + "_decode_attention",
    )(page_table, q, k, v, *([ck] * pps), *([cv] * pps), *extra)


def _outproj_kernel(o_ref, w_ref, x_ref, g_ref, y_ref):
    y_ref[...] = x_ref[...] + g_ref[...] * _dot(o_ref[...].astype(BF16), w_ref[...])


def _outproj(o, w_bf16, x, gate):
    t, d = x.shape
    tm = _pick(t, (512, 256, 128))
    row = pl.BlockSpec((tm, d), lambda i: (i, 0))
    return pl.pallas_call(
        _outproj_kernel,
        grid=(t // tm,),
        in_specs=[row, pl.BlockSpec((d, d), lambda i: (0, 0)), row, row],
        out_specs=row,
        out_shape=jax.ShapeDtypeStruct((t, d), F32),
        compiler_params=_cparams("arbitrary"),
        name="outproj_residual",
    )(o, w_bf16, x, gate)


def _peer_scores_kernel(x_ref, g_ref, sc_ref, sh_ref, wqt_ref, keys_ref, h_ref, s_ref):
    h = _norm_mod(x_ref[...], g_ref[...], sc_ref[...], sh_ref[...]).astype(BF16)
    h_ref[...] = h
    qt = _dot_nt(wqt_ref[...], h)
    n_hc = s_ref.shape[0]
    half = keys_ref.shape[2]
    for hc in range(n_hc):
        s_ref[hc] = _dot(keys_ref[hc % 2], qt[hc * half:(hc + 1) * half].astype(BF16))


def _peer_scores(x, gain, sc, sh, wqt_bf16, keys_bf16):
    t, d = x.shape
    nq = wqt_bf16.shape[0]
    half = keys_bf16.shape[2]
    n_hc = nq // half
    tm = _pick(t, (256, 128))
    row = pl.BlockSpec((tm, d), lambda i: (i, 0))
    return pl.pallas_call(
        _peer_scores_kernel,
        grid=(t // tm,),
        in_specs=[row, pl.BlockSpec((1, d), lambda i: (0, 0)), row, row,
                  pl.BlockSpec((nq, d), lambda i: (0, 0)),
                  pl.BlockSpec(keys_bf16.shape, lambda i: (0, 0, 0))],
        out_specs=[row, pl.BlockSpec((n_hc, N_KEYS, tm), lambda i: (0, 0, i))],
        out_shape=[jax.ShapeDtypeStruct((t, d), BF16),
                   jax.ShapeDtypeStruct((n_hc, N_KEYS, t), F32)],
        compiler_params=_cparams("arbitrary"),
        name="peer_scores",
    )(x, gain.reshape(1, d), sc, sh, wqt_bf16, keys_bf16)


def _extract_topk(vals, k, break_ties):
    n = vals.shape[0]
    idx = lax.broadcasted_iota(jnp.int32, vals.shape, 0) if break_ties else None
    rank = jnp.full(vals.shape, float(k), F32)
    tops = []
    for r in range(k):
        m = jnp.max(vals, axis=0, keepdims=True)
        hit = vals == m
        if break_ties:
            first = jnp.min(jnp.where(hit, idx, n), axis=0, keepdims=True)
            hit = idx == first
        vals = jnp.where(hit, -jnp.inf, vals)
        rank = jnp.where(hit, float(r), rank)
        tops.append(m)
    return tops, rank


def _peer_select(s1, s2, k, break_ties):
    half = k // 2
    assert half % SUBLANES == 0
    t1, rank1 = _extract_topk(s1, k, break_ties)
    t2, rank2 = _extract_topk(s2, k, break_ties)
    top1 = jnp.concatenate(t1, axis=0)
    top2 = jnp.concatenate(t2, axis=0)
    cand = jnp.concatenate(
        [t1[0] + top2] + [t1[r] + top2[:half] for r in range(1, half)] + [top1[half:] + t2[0]],
        axis=0)
    tops, rank_c = _extract_topk(cand, k, break_ties)
    chosen = jnp.where(rank_c < float(k), 1.0, 0.0)
    counts = [jnp.sum(chosen[:k], axis=0, keepdims=True)]
    for r in range(1, half):
        lo = k + (r - 1) * half
        counts.append(jnp.sum(chosen[lo:lo + half], axis=0, keepdims=True))
    lo = k + (half - 1) * half
    counts += [chosen[lo + r:lo + r + 1] for r in range(half)]
    z = jnp.zeros_like(tops[0])
    for j in range(k):
        z = z + jnp.exp(tops[j] - tops[0])
    lrow = jnp.zeros(s1.shape, F32)
    for r in range(k):
        lrow = jnp.where(rank1 == float(r), counts[r], lrow)
    in1 = rank1 < float(k)
    in2 = rank2 < float(k)
    w1 = jnp.where(in1, jnp.exp(s1 - t1[0]), 0.0)
    w2 = jnp.where(in2, jnp.exp(s2 - t2[0]), 0.0) / z
    n_sel = (jnp.sum(jnp.where(in1, 1.0, 0.0), axis=0, keepdims=True)
             + jnp.sum(jnp.where(in2, 1.0, 0.0), axis=0, keepdims=True)
             + jnp.sum(chosen, axis=0, keepdims=True))
    return rank2, w2, lrow, w1, n_sel


def _peer_topk_kernel(s_ref, r2_ref, w2_ref, lr_ref, w1_ref):
    k = PEER_TOPK
    s1, s2 = s_ref[0], s_ref[1]

    def run(break_ties):
        rank2, w2, lrow, w1, n_sel = _peer_select(s1, s2, k, break_ties)
        r2_ref[0] = rank2.astype(r2_ref.dtype)
        w2_ref[0] = w2.astype(w2_ref.dtype)
        lr_ref[0] = lrow
        w1_ref[0] = w1
        return n_sel

    n_sel = run(False)
    @pl.when(jnp.max(jnp.abs(n_sel - 3.0 * k)) > 0.0)
    def _():
        run(True)


def _peer_topk(s):
    n_hc, n_keys, t = s.shape
    heads = n_hc // 2
    tb = LANES
    out = pl.BlockSpec((1, n_keys, tb), lambda i, h: (h, 0, i))
    return pl.pallas_call(
        _peer_topk_kernel,
        grid=(t // tb, heads),
        in_specs=[pl.BlockSpec((2, n_keys, tb), lambda i, h: (h, 0, i))],
        out_specs=[out] * 4,
        out_shape=[jax.ShapeDtypeStruct((heads, n_keys, t), dt) for dt in (BF16, BF16, F32, F32)],
        compiler_params=_cparams("arbitrary", "arbitrary"),
        name="peer_topk",
    )(s)


def _gelu_tanh(a):
    return 0.5 * a * (1.0 + jnp.tanh(math.sqrt(2.0 / math.pi) * (a + 0.044715 * (a * a * a))))


def _peer_expert_kernel(h_ref, r2_ref, w2_ref, lr_ref, w1_ref, u_ref, vt_ref, x_ref, g_ref,
                        o_ref, acc_sc, act_sc, coef_sc):
    e = pl.program_id(1)
    heads, n_keys, _ = r2_ref.shape
    rows_per_step = u_ref.shape[0] // n_keys

    @pl.when(e == 0)
    def _():
        acc_sc[...] = jnp.zeros_like(acc_sc)

    act_sc[...] = _dot_nt(u_ref[...], h_ref[...])
    for kk in range(rows_per_step):
        i1 = e * rows_per_step + kk
        rows = slice(kk * n_keys, (kk + 1) * n_keys)
        lrows = [lr_ref[hd, pl.ds(i1, 1), :].astype(BF16) for hd in range(heads)]
        w1rows = [w1_ref[hd, pl.ds(i1, 1), :].astype(BF16) for hd in range(heads)]
        for c in range(act_sc.shape[1] // LANES):
            cols = slice(c * LANES, (c + 1) * LANES)
            cmat = None
            for hd in range(heads):
                keep = r2_ref[hd, :, cols] < lrows[hd][:, cols]
                term = jnp.where(keep, w2_ref[hd, :, cols], 0.0) * w1rows[hd][:, cols]
                cmat = term if cmat is None else cmat + term
            coef_sc[rows, cols] = cmat * _gelu_tanh(act_sc[rows, cols]).astype(BF16)
    acc_sc[...] += _dot(vt_ref[...], coef_sc[...])

    @pl.when(e == pl.num_programs(1) - 1)
    def _():
        o_ref[...] = x_ref[...] + g_ref[...] * jnp.transpose(acc_sc[...])


def _peer_expert(h, r2, w2, lr, w1, u_bf16, vt_bf16, x, gate):
    t, d = x.shape
    heads, n_keys, _ = r2.shape
    n_exp = u_bf16.shape[0]
    tb = _pick(t, (512, 256, 128))
    eb = 512
    row = pl.BlockSpec((tb, d), lambda i, e: (i, 0))
    sel = pl.BlockSpec((heads, n_keys, tb), lambda i, e: (0, 0, i))
    return pl.pallas_call(
        _peer_expert_kernel,
        grid=(t // tb, n_exp // eb),
        in_specs=[row, sel, sel, sel, sel,
                  pl.BlockSpec((eb, d), lambda i, e: (e, 0)),
                  pl.BlockSpec((d, eb), lambda i, e: (0, e)),
                  row, row],
        out_specs=row,
        out_shape=jax.ShapeDtypeStruct((t, d), F32),
        scratch_shapes=[pltpu.VMEM((d, tb), F32), pltpu.VMEM((eb, tb), F32),
                        pltpu.VMEM((eb, tb), BF16)],
        compiler_params=_cparams("arbitrary", "arbitrary"),
        name="peer_experts",
    )(h, r2, w2, lr, w1, u_bf16, vt_bf16, x, gate)


def _peer(x, gain, sc, sh, gate, w_query, sub_keys, expert_u, expert_v):
    h, s = _peer_scores(x, gain, sc, sh, jnp.transpose(w_query).astype(BF16),
                        sub_keys.astype(BF16))
    r2, w2, lr, w1 = _peer_topk(s)
    return _peer_expert(h, r2, w2, lr, w1, expert_u.astype(BF16),
                        jnp.transpose(expert_v).astype(BF16), x, gate)


def kernel(x_prompt, x_sample, cache_k_sb, cache_v_sb, cache_k_diff, cache_v_diff, page_table, c_prompt, c_sample, w_ada, b_ada, norm_mix, norm_ffn, w_qkv_sb, w_o_sb, w_qkv_diff, w_o_diff, diff_q_gain, diff_k_gain, diff_lambda_q1, diff_lambda_k1, diff_lambda_q2, diff_lambda_k2, diff_sub_gain, peer_w_query, peer_sub_keys, peer_u, peer_v):
    batch, seq, d = x_prompt.shape
    n_seq, n_new, _ = x_sample.shape
    assert d == SB_HEADS * HEAD_DIM == DIFF_HEADS * 2 * HEAD_DIM and n_new == SUBLANES
    depth = w_ada.shape[0]
    tp, ts = batch * seq, n_seq * n_new
    past_len = page_table.shape[1] * cache_k_sb.shape[2]

    x = jnp.concatenate([x_prompt.reshape(tp, d), x_sample.reshape(ts, d)], axis=0)
    c_all = jnp.concatenate([c_prompt, c_sample], axis=0)
    n_c = c_all.shape[0]
    c_all = jnp.pad(c_all, ((0, (-n_c) % SUBLANES), (0, 0)))
    mod = _ada(c_all, w_ada, b_ada)

    def per_token(layer, j):
        m = mod[layer, :, j * d:(j + 1) * d]
        return jnp.concatenate([jnp.repeat(m[:batch], seq, axis=0),
                                jnp.repeat(m[batch:n_c], n_new, axis=0)], axis=0)

    pos = jnp.concatenate([jnp.tile(jnp.arange(seq), batch),
                           jnp.tile(past_len + jnp.arange(n_new), n_seq)])
    cos_t, s1_t, s2_t = _rope_tables(pos)

    sb_k, sb_v, df_k, df_v = [], [], [], []
    for i in range(depth):
        sh_m, sc_m, g_m, sh_f, sc_f, g_f = [per_token(i, j) for j in range(6)]
        j = i // 2
        if i % 2 == 0:
            q, k, v = _qkv(x, norm_mix[i], sc_m, sh_m, w_qkv_sb[j].astype(BF16))
            o_p = _sb_prompt(q, k, v, batch, seq)
            o_s = _decode_attention("sb", q, k, v, cache_k_sb, cache_v_sb, j, page_table, tp,
                                    scale=HEAD_DIM ** -0.5)
            w_o = w_o_sb[j]
            sb_k.append(k)
            sb_v.append(v)
        else:
            lambda_init = 0.8 - 0.6 * math.exp(-0.3 * i)
            q, k, v = _qkv(x, norm_mix[i], sc_m, sh_m, w_qkv_diff[j].astype(BF16))
            q, k = _qknorm_rope(q, k, diff_q_gain[j], diff_k_gain[j], cos_t, s1_t, s2_t)
            lam_vecs = (diff_lambda_q1[j], diff_lambda_k1[j], diff_lambda_q2[j], diff_lambda_k2[j])
            o_p = _diff_prompt(q, k, v, lam_vecs, diff_sub_gain[j], lambda_init, batch, seq)
            o_s = _decode_attention(
                "diff", q, k, v, cache_k_diff, cache_v_diff, j, page_table, tp,
                extra=tuple(a.reshape(1, HEAD_DIM) for a in lam_vecs)
                + (diff_sub_gain[j].reshape(1, LANES),),
                scale=HEAD_DIM ** -0.5, lambda_init=lambda_init)
            w_o = w_o_diff[j]
            df_k.append(k)
            df_v.append(v)
        x = _outproj(jnp.concatenate([o_p, o_s], axis=0), w_o.astype(BF16), x, g_m)
        x = _peer(x, norm_ffn[i], sc_f, sh_f, g_f, peer_w_query[i], peer_sub_keys[i],
                  peer_u[i], peer_v[i])

    def rows(stack, lo, hi, shape):
        return jnp.stack([a[lo:hi].reshape(shape) for a in stack])

    t = tp + ts
    return (
        x[:tp].reshape(batch, seq, d),
        x[tp:].reshape(n_seq, n_new, d),
        rows(sb_k, 0, tp, (batch, seq, SB_HEADS, HEAD_DIM)),
        rows(sb_v, 0, tp, (batch, seq, SB_HEADS, HEAD_DIM)),
        rows(df_k, 0, tp, (batch, seq, DIFF_HEADS, 2, HEAD_DIM)),
        rows(df_v, 0, tp, (batch, seq, DIFF_HEADS, 2 * HEAD_DIM)),
        rows(sb_k, tp, t, (n_seq, n_new, SB_HEADS, HEAD_DIM)),
        rows(sb_v, tp, t, (n_seq, n_new, SB_HEADS, HEAD_DIM)),
        rows(df_k, tp, t, (n_seq, n_new, DIFF_HEADS, 2, HEAD_DIM)),
        rows(df_v, tp, t, (n_seq, n_new, DIFF_HEADS, 2 * HEAD_DIM)),
    )
```

```python
import functools
import math

import jax
import jax.numpy as jnp
from jax import lax
from jax.experimental import pallas as pl
from jax.experimental.pallas import tpu as pltpu

F32 = jnp.float32
BF16 = jnp.bfloat16

RMS_EPS = 1e-6
ROPE_THETA = 500000.0
SB_HEADS = 16
DIFF_HEADS = 8
HEAD_DIM = 64
ROPE_DIM = HEAD_DIM // 4
N_KEYS = 128
PEER_HEADS = 8
PEER_TOPK = 16
LANES = 128
SUBLANES = 8
VMEM_LIMIT = 56 * 1024 * 1024
EXP_ZERO_BELOW = -105.0


def _cparams(*sem):
    return pltpu.CompilerParams(dimension_semantics=sem, vmem_limit_bytes=VMEM_LIMIT)


def _pick(n, choices):
    for c in choices:
        if n % c == 0:
            return c
    raise ValueError(f"no block size in {choices} divides {n}")


def _dot(a, b):
    return jnp.dot(a, b, preferred_element_type=F32)


def _dot_nt(a, b):
    return lax.dot_general(a, b, (((1,), (1,)), ((), ())), preferred_element_type=F32)


def _split3(a):
    hi = a.astype(BF16)
    r = a - hi.astype(F32)
    mid = r.astype(BF16)
    lo = (r - mid.astype(F32)).astype(BF16)
    return hi, mid, lo


def _dot_f32_lhs(a, m):
    hi, mid, lo = _split3(a)
    return _dot(hi, m) + _dot(mid, m) + _dot(lo, m)


def _dot_f32_rhs(m, a):
    hi, mid, lo = _split3(a)
    return _dot(m, hi) + _dot(m, mid) + _dot(m, lo)


def _softplus(z):
    return jnp.maximum(z, 0.0) + jnp.log(1.0 + jnp.exp(-jnp.abs(z)))


def _norm_mod(x, gain, sc, sh):
    ms = jnp.mean(x * x, axis=-1, keepdims=True)
    y = x * lax.rsqrt(ms + RMS_EPS)
    return (y * gain) * (1.0 + sc) + sh


def _ada_kernel(c_ref, w_ref, b_ref, o_ref):
    c = c_ref[...]
    s = c / (1.0 + jnp.exp(-c))
    o_ref[0] = _dot(s.astype(BF16), w_ref[0].astype(BF16)) + b_ref[0]


def _ada(c_all, w_ada, b_ada):
    n_layers, d, n = w_ada.shape
    r = c_all.shape[0]
    tn = d
    return pl.pallas_call(
        _ada_kernel,
        grid=(n_layers, n // tn),
        in_specs=[
            pl.BlockSpec((r, d), lambda l, j: (0, 0)),
            pl.BlockSpec((1, d, tn), lambda l, j: (l, 0, j)),
            pl.BlockSpec((1, 1, tn), lambda l, j: (l, 0, j)),
        ],
        out_specs=pl.BlockSpec((1, r, tn), lambda l, j: (l, 0, j)),
        out_shape=jax.ShapeDtypeStruct((n_layers, r, n), F32),
        compiler_params=_cparams("arbitrary", "arbitrary"),
        name="ada_modulation",
    )(c_all, w_ada, b_ada.reshape(n_layers, 1, n))


def _qkv_kernel(x_ref, g_ref, sc_ref, sh_ref, w_ref, q_ref, k_ref, v_ref):
    d = x_ref.shape[1]
    h = _norm_mod(x_ref[...], g_ref[...], sc_ref[...], sh_ref[...]).astype(BF16)
    r = _dot(h, w_ref[...])
    q_ref[...] = r[:, :d]
    k_ref[...] = r[:, d:2 * d]
    v_ref[...] = r[:, 2 * d:]


def _qkv(x, gain, sc, sh, w_bf16):
    t, d = x.shape
    tm = _pick(t, (256, 128))
    row = pl.BlockSpec((tm, d), lambda i: (i, 0))
    return pl.pallas_call(
        _qkv_kernel,
        grid=(t // tm,),
        in_specs=[row, pl.BlockSpec((1, d), lambda i: (0, 0)), row, row,
                  pl.BlockSpec((d, 3 * d), lambda i: (0, 0))],
        out_specs=[row, row, row],
        out_shape=[jax.ShapeDtypeStruct((t, d), F32)] * 3,
        compiler_params=_cparams("arbitrary"),
        name="norm_qkv",
    )(x, gain.reshape(1, d), sc, sh, w_bf16)


def _qknorm_rope_kernel(q_ref, k_ref, qg_ref, kg_ref, c_ref, s1_ref, s2_ref, qo_ref, ko_ref):
    d = q_ref.shape[1]
    r = lax.broadcasted_iota(jnp.int32, (LANES, LANES), 0) // HEAD_DIM
    c = lax.broadcasted_iota(jnp.int32, (LANES, LANES), 1) // HEAD_DIM
    seg = jnp.where(r == c, 1.0, 0.0).astype(BF16)
    cosv, s1, s2 = c_ref[...], s1_ref[...], s2_ref[...]
    for src, gain, dst in ((q_ref, qg_ref, qo_ref), (k_ref, kg_ref, ko_ref)):
        g = gain[...]
        for j in range(d // LANES):
            x = src[:, j * LANES:(j + 1) * LANES]
            ms = _dot_f32_lhs(x * x, seg) * (1.0 / HEAD_DIM)
            y = x * lax.rsqrt(ms + RMS_EPS) * g
            up = pltpu.roll(y, LANES - ROPE_DIM // 2, 1)
            dn = pltpu.roll(y, ROPE_DIM // 2, 1)
            dst[:, j * LANES:(j + 1) * LANES] = y * cosv + up * s1 + dn * s2


def _qknorm_rope(q, k, q_gain, k_gain, cos_t, s1_t, s2_t):
    t, d = q.shape
    tm = _pick(t, (512, 256, 128))
    row = pl.BlockSpec((tm, d), lambda i: (i, 0))
    tab = pl.BlockSpec((tm, LANES), lambda i: (i, 0))
    gs = pl.BlockSpec((1, LANES), lambda i: (0, 0))
    tile = LANES // HEAD_DIM
    return pl.pallas_call(
        _qknorm_rope_kernel,
        grid=(t // tm,),
        in_specs=[row, row, gs, gs, tab, tab, tab],
        out_specs=[row, row],
        out_shape=[jax.ShapeDtypeStruct((t, d), F32)] * 2,
        compiler_params=_cparams("arbitrary"),
        name="qknorm_rope",
    )(q, k, jnp.tile(q_gain, tile).reshape(1, LANES), jnp.tile(k_gain, tile).reshape(1, LANES),
      cos_t, s1_t, s2_t)


def _rope_tables(pos):
    half = ROPE_DIM // 2
    inv = ROPE_THETA ** (-jnp.arange(half, dtype=F32) / half)
    ang = pos.astype(F32)[:, None] * inv[None, :]
    cos, sin = jnp.cos(ang), jnp.sin(ang)
    t = pos.shape[0]
    pad = jnp.zeros((t, HEAD_DIM - ROPE_DIM), F32)
    z = jnp.zeros((t, half), F32)
    c = jnp.concatenate([cos, cos, pad + 1.0], axis=1)
    s1 = jnp.concatenate([-sin, z, pad], axis=1)
    s2 = jnp.concatenate([z, sin, pad], axis=1)
    rep = LANES // HEAD_DIM
    return jnp.tile(c, (1, rep)), jnp.tile(s1, (1, rep)), jnp.tile(s2, (1, rep))


def _sb_prompt_kernel(q_ref, k_ref, v_ref, o_ref, *, tq, tk, scale):
    i = pl.program_id(2)
    q0 = i * tq
    jr = lax.broadcasted_iota(jnp.int32, (tk, tk), 0)
    jc = lax.broadcasted_iota(jnp.int32, (tk, tk), 1)
    later = jnp.where(jr > jc, 1.0, 0.0).astype(BF16)
    row = q0 + lax.broadcasted_iota(jnp.int32, (tq, 1), 0)
    n_kb = (q0 + tq) // tk
    n_heads = LANES // HEAD_DIM
    qs = [(q_ref[:, hh * HEAD_DIM:(hh + 1) * HEAD_DIM] * scale).astype(BF16)
          for hh in range(n_heads)]

    def cond(carry):
        kb, state = carry
        live = functools.reduce(jnp.maximum, [jnp.max(cs) for _, cs in state])
        return jnp.logical_and(kb >= 0, live > EXP_ZERO_BELOW)

    def body(carry):
        kb, state = carry
        k0 = pl.multiple_of(kb * tk, tk)
        kblk = k_ref[pl.ds(k0, tk), :].astype(BF16)
        vblk = v_ref[pl.ds(k0, tk), :].astype(BF16)
        col = k0 + lax.broadcasted_iota(jnp.int32, (1, tk), 1)
        causal = col < row
        new = []
        for hh, (acc, cs) in enumerate(state):
            lo, hi = hh * HEAD_DIM, (hh + 1) * HEAD_DIM
            z = _dot_nt(qs[hh], kblk[:, lo:hi])
            sp = _softplus(z)
            log_keep = jnp.where(causal, -sp, 0.0)
            after = _dot_f32_lhs(log_keep, later) + cs
            w = jnp.where(causal, jnp.exp((z - sp) + after), 0.0)
            acc = acc + _dot(w.astype(BF16), vblk[:, lo:hi])
            cs = cs + jnp.sum(log_keep, axis=1, keepdims=True)
            new.append((acc, cs))
        return kb - 1, tuple(new)

    init = tuple((jnp.zeros((tq, HEAD_DIM), F32), jnp.zeros((tq, 1), F32))
                 for _ in range(n_heads))
    _, state = lax.while_loop(cond, body, (n_kb - 1, init))
    o_ref[...] = jnp.concatenate([acc for acc, _ in state], axis=1)


def _sb_prompt(q, k, v, batch, seq):
    d = q.shape[1]
    tq = _pick(seq, (256, 128))
    tk = tq
    nq = seq // tq
    return pl.pallas_call(
        functools.partial(_sb_prompt_kernel, tq=tq, tk=tk, scale=HEAD_DIM ** -0.5),
        grid=(batch, d // LANES, nq),
        in_specs=[
            pl.BlockSpec((tq, LANES), lambda b, h, i: (b * nq + i, h)),
            pl.BlockSpec((seq, LANES), lambda b, h, i: (b, h)),
            pl.BlockSpec((seq, LANES), lambda b, h, i: (b, h)),
        ],
        out_specs=pl.BlockSpec((tq, LANES), lambda b, h, i: (b * nq + i, h)),
        out_shape=jax.ShapeDtypeStruct((batch * seq, d), F32),
        compiler_params=_cparams("arbitrary", "arbitrary", "arbitrary"),
        name="sb_prompt_attention",
    )(q, k, v)


def _lambda(lq1_ref, lk1_ref, lq2_ref, lk2_ref, lambda_init):
    a = jnp.sum(lq1_ref[...] * lk1_ref[...], axis=1, keepdims=True)
    b = jnp.sum(lq2_ref[...] * lk2_ref[...], axis=1, keepdims=True)
    return jnp.exp(a) - jnp.exp(b) + lambda_init


def _diff_prompt_kernel(q_ref, k_ref, v_ref, lq1_ref, lk1_ref, lq2_ref, lk2_ref, sg_ref, o_ref,
                        *, tq, scale, lambda_init):
    i = pl.program_id(2)
    q0 = i * tq
    tk = tq
    lam = _lambda(lq1_ref, lk1_ref, lq2_ref, lk2_ref, lambda_init)
    qs = [(q_ref[:, c * HEAD_DIM:(c + 1) * HEAD_DIM] * scale).astype(BF16) for c in range(2)]
    row = lax.broadcasted_iota(jnp.int32, (tq, 1), 0)
    col = lax.broadcasted_iota(jnp.int32, (1, tk), 1)
    diag_mask = col <= row

    def step(kb, carry, masked):
        k0 = pl.multiple_of(kb * tk, tk)
        vblk = v_ref[pl.ds(k0, tk), :].astype(BF16)
        vext = jnp.concatenate([vblk, jnp.ones_like(vblk)], axis=1)
        new = []
        for c in range(2):
            m, acc = carry[c]
            kblk = k_ref[pl.ds(k0, tk), c * HEAD_DIM:(c + 1) * HEAD_DIM].astype(BF16)
            s = _dot_nt(qs[c], kblk)
            if masked:
                s = jnp.where(diag_mask, s, -jnp.inf)
            m_new = jnp.maximum(m, jnp.max(s, axis=1, keepdims=True))
            p = jnp.exp(s - m_new)
            acc = acc * jnp.exp(m - m_new) + _dot(p.astype(BF16), vext)
            new.append((m_new, acc))
        return tuple(new)

    init = tuple((jnp.full((tq, 1), -jnp.inf, F32), jnp.zeros((tq, 2 * LANES), F32))
                 for _ in range(2))
    carry = lax.fori_loop(0, i, lambda kb, cr: step(kb, cr, False), init)
    (_, a0), (_, a1) = step(i, carry, True)
    o = a0[:, :LANES] / a0[:, LANES:] - lam * (a1[:, :LANES] / a1[:, LANES:])
    ms = jnp.mean(o * o, axis=-1, keepdims=True)
    o_ref[...] = (o * lax.rsqrt(ms + RMS_EPS) * sg_ref[...]) * (1.0 - lambda_init)


def _diff_prompt(q, k, v, lam_vecs, sub_gain, lambda_init, batch, seq):
    d = q.shape[1]
    tq = _pick(seq, (512, 256, 128))
    nq = seq // tq
    vec = pl.BlockSpec((1, HEAD_DIM), lambda b, h, i: (0, 0))
    return pl.pallas_call(
        functools.partial(_diff_prompt_kernel, tq=tq, scale=HEAD_DIM ** -0.5,
                          lambda_init=lambda_init),
        grid=(batch, d // LANES, nq),
        in_specs=[
            pl.BlockSpec((tq, LANES), lambda b, h, i: (b * nq + i, h)),
            pl.BlockSpec((seq, LANES), lambda b, h, i: (b, h)),
            pl.BlockSpec((seq, LANES), lambda b, h, i: (b, h)),
            vec, vec, vec, vec,
            pl.BlockSpec((1, LANES), lambda b, h, i: (0, 0)),
        ],
        out_specs=pl.BlockSpec((tq, LANES), lambda b, h, i: (b * nq + i, h)),
        out_shape=jax.ShapeDtypeStruct((batch * seq, d), F32),
        compiler_params=_cparams("arbitrary", "arbitrary", "arbitrary"),
        name="diff_prompt_attention",
    )(q, k, v, *[x.reshape(1, HEAD_DIM) for x in lam_vecs], sub_gain.reshape(1, LANES))


def _block_diag_queries(q, scale):
    n_q, d = q.shape
    groups = d // HEAD_DIM
    rep = jnp.concatenate([q * scale] * groups, axis=0)
    rg = lax.broadcasted_iota(jnp.int32, rep.shape, 0) // n_q
    cg = lax.broadcasted_iota(jnp.int32, rep.shape, 1) // HEAD_DIM
    return jnp.where(rg == cg, rep, 0.0).astype(BF16)


def _pad_rows(x, rows):
    return jnp.concatenate([x, jnp.zeros((rows - x.shape[0], x.shape[1]), x.dtype)], axis=0)


def _take_group_blocks(acc, n_q, width):
    rows, d = acc.shape
    a3 = acc.reshape(rows // n_q, n_q, d)
    g = lax.broadcasted_iota(jnp.int32, a3.shape, 0)
    cg = lax.broadcasted_iota(jnp.int32, a3.shape, 2) // width
    return jnp.sum(jnp.where(g == cg, a3, 0.0), axis=0)


def _load_rows_page(ref, page):
    groups = ref.shape[0] // page
    pieces = [ref[pl.ds(g, page, stride=groups), :] for g in range(groups)]
    return jnp.concatenate(pieces, axis=1)


def _new_rows_block(qbd, kn_ref, vn_ref, page, strict):
    n_q = kn_ref.shape[0]
    k_pad = _pad_rows(kn_ref[...], page).astype(BF16)
    v_pad = _pad_rows(vn_ref[...], page).astype(BF16)
    qi = lax.broadcasted_iota(jnp.int32, (LANES, page), 0) % n_q
    key = lax.broadcasted_iota(jnp.int32, (LANES, page), 1)
    valid = key < qi if strict else key <= qi
    return _dot_nt(qbd, k_pad), lambda w: _dot(w.astype(BF16), v_pad), valid


def _page_scores(qbd, kt_ref):
    return _dot(qbd, kt_ref[...].astype(BF16))


def _sb_blocks(blocks, cs):
    n = blocks[0][0].shape[1]
    jr = lax.broadcasted_iota(jnp.int32, (n, n), 0)
    jc = lax.broadcasted_iota(jnp.int32, (n, n), 1)
    later = jnp.where(jr > jc, 1.0, 0.0).astype(BF16)
    out = None
    for z, values, valid in blocks:
        sp = _softplus(z)
        log_keep = -sp if valid is None else jnp.where(valid, -sp, 0.0)
        after = _dot_f32_lhs(log_keep, later) + cs
        w = jnp.exp((z - sp) + after)
        if valid is not None:
            w = jnp.where(valid, w, 0.0)
        c = values(w)
        out = c if out is None else out + c
        cs = cs + jnp.sum(log_keep, axis=1, keepdims=True)
    return out, cs


def _sb_decode_kernel(pt_ref, q_ref, kn_ref, vn_ref, *rest, scale, pps, page):
    kp, vp, o_ref = rest[:pps], rest[pps:2 * pps], rest[2 * pps]
    qbd_sc, acc_sc, cs_sc = rest[2 * pps + 1:]
    s = pl.program_id(1)
    n_q = q_ref.shape[0]

    @pl.when(s == 0)
    def _():
        qbd = _block_diag_queries(q_ref[...], scale)
        qbd_sc[...] = qbd
        new = _new_rows_block(qbd, kn_ref, vn_ref, page, True)
        acc_sc[...], cs_sc[...] = _sb_blocks([new], jnp.zeros(cs_sc.shape, F32))

    @pl.when(jnp.max(cs_sc[...]) > EXP_ZERO_BELOW)
    def _():
        qbd = qbd_sc[...]
        blocks = [(_page_scores(qbd, kp[i]),
                   lambda w, i=i: _dot_nt(w.astype(BF16), vp[i][...].astype(BF16)), None)
                  for i in range(pps)]
        out, cs = _sb_blocks(blocks, cs_sc[...])
        acc_sc[...] += out
        cs_sc[...] = cs

    @pl.when(s == pl.num_programs(1) - 1)
    def _():
        o_ref[...] = _take_group_blocks(acc_sc[...], n_q, HEAD_DIM)


def _diff_blocks(blocks, m_old, l_old, acc_old):
    zs = [z if valid is None else jnp.where(valid, z, -jnp.inf) for z, _, valid in blocks]
    m_new = functools.reduce(jnp.maximum, [jnp.max(z, axis=1, keepdims=True) for z in zs])
    if acc_old is None:
        l, acc = jnp.zeros_like(m_new), None
    else:
        m_new = jnp.maximum(m_old, m_new)
        alpha = jnp.exp(m_old - m_new)
        l, acc = l_old * alpha, acc_old * alpha
    for (_, values, _), z in zip(blocks, zs):
        p = jnp.exp(z - m_new)
        l = l + jnp.sum(p, axis=1, keepdims=True)
        c = values(p)
        acc = c if acc is None else acc + c
    return m_new, l, acc


def _diff_decode_kernel(pt_ref, q_ref, kn_ref, vn_ref, *rest, scale, lambda_init, pps, page):
    kp, vp = rest[:pps], rest[pps:2 * pps]
    lq1_ref, lk1_ref, lq2_ref, lk2_ref, sg_ref, o_ref = rest[2 * pps:2 * pps + 6]
    qbd_sc, acc_sc, m_sc, l_sc = rest[2 * pps + 6:]
    s = pl.program_id(1)
    n_q = q_ref.shape[0]

    @pl.when(s == 0)
    def _():
        qbd = _block_diag_queries(q_ref[...], scale)
        qbd_sc[...] = qbd
        new = _new_rows_block(qbd, kn_ref, vn_ref, page, False)
        m_sc[...], l_sc[...], acc_sc[...] = _diff_blocks([new], None, None, None)

    qbd = qbd_sc[...]
    blocks = [(_page_scores(qbd, kp[i]),
               lambda w, i=i: _dot(w.astype(BF16), _load_rows_page(vp[i], page).astype(BF16)), None)
              for i in range(pps)]
    m_sc[...], l_sc[...], acc_sc[...] = _diff_blocks(blocks, m_sc[...], l_sc[...], acc_sc[...])

    @pl.when(s == pl.num_programs(1) - 1)
    def _():
        lam = _lambda(lq1_ref, lk1_ref, lq2_ref, lk2_ref, lambda_init)
        d = acc_sc.shape[1]
        full = acc_sc[...] * (1.0 / l_sc[...])
        a4 = full.reshape(DIFF_HEADS, 2, n_q, d)
        hd = lax.broadcasted_iota(jnp.int32, (DIFF_HEADS, n_q, d), 0)
        cg = lax.broadcasted_iota(jnp.int32, (DIFF_HEADS, n_q, d), 2) // LANES
        o0 = jnp.sum(jnp.where(hd == cg, a4[:, 0], 0.0), axis=0)
        o1 = jnp.sum(jnp.where(hd == cg, a4[:, 1], 0.0), axis=0)
        o = o0 - lam * o1
        sg = sg_ref[...]
        for j in range(d // LANES):
            blk = o[:, j * LANES:(j + 1) * LANES]
            ms = jnp.mean(blk * blk, axis=-1, keepdims=True)
            o_ref[:, j * LANES:(j + 1) * LANES] = (
                (blk * lax.rsqrt(ms + RMS_EPS) * sg) * (1.0 - lambda_init))


def _transposed_pages(cache):
    t = jnp.transpose(cache, (0, 1) + tuple(range(3, cache.ndim)) + (2,))
    return t.reshape(t.shape[0], t.shape[1], -1, t.shape[-1])


def _decode_attention(kind, q, k, v, cache_k, cache_v, layer, page_table, row0, extra=(), **kw):
    d = q.shape[1]
    n_seq, n_pages = page_table.shape
    n_layers, n_phys, page = cache_k.shape[:3]
    n_q = SUBLANES
    blk0 = row0 // n_q
    pps = _pick(n_pages, (4, 2, 1))
    new = pl.BlockSpec((n_q, d), lambda b, s, pt: (blk0 + b, 0))

    def page_specs(cache):
        return [pl.BlockSpec((None, None) + cache.shape[2:],
                             lambda b, s, pt, i=i: (layer, pt[b, n_pages - 1 - (s * pps + i)], 0, 0))
                for i in range(pps)]

    cache_k = _transposed_pages(cache_k)
    kw = dict(kw, pps=pps, page=page)
    col = pltpu.VMEM((LANES, 1), F32)
    if kind == "sb":
        cache_v = _transposed_pages(cache_v)
        body = functools.partial(_sb_decode_kernel, **kw)
        extra_specs = []
        scratch = [pltpu.VMEM((LANES, d), BF16), pltpu.VMEM((LANES, d), F32), col]
    else:
        cache_v = cache_v.reshape(n_layers, n_phys, -1, cache_v.shape[-1])
        body = functools.partial(_diff_decode_kernel, **kw)
        vec = pl.BlockSpec((1, HEAD_DIM), lambda b, s, pt: (0, 0))
        extra_specs = [vec, vec, vec, vec, pl.BlockSpec((1, LANES), lambda b, s, pt: (0, 0))]
        scratch = [pltpu.VMEM((LANES, d), BF16), pltpu.VMEM((LANES, d), F32), col, col]
    return pl.pallas_call(
        body,
        grid_spec=pltpu.PrefetchScalarGridSpec(
            num_scalar_prefetch=1,
            grid=(n_seq, n_pages // pps),
            in_specs=[new, new, new] + page_specs(cache_k) + page_specs(cache_v) + extra_specs,
            out_specs=pl.BlockSpec((n_q, d), lambda b, s, pt: (b, 0)),
            scratch_shapes=scratch,
        ),
        out_shape=jax.ShapeDtypeStruct((n_seq * n_q, d), F32),
        compiler_params=_cparams("arbitrary", "arbitrary"),
        name=kind + "_decode_attention",
    )(page_table, q, k, v, *([cache_k] * pps), *([cache_v] * pps), *extra)


def _outproj_kernel(o_ref, w_ref, x_ref, g_ref, y_ref):
    y_ref[...] = x_ref[...] + g_ref[...] * _dot(o_ref[...].astype(BF16), w_ref[...])


def _outproj(o, w_bf16, x, gate):
    t, d = x.shape
    tm = _pick(t, (512, 256, 128))
    row = pl.BlockSpec((tm, d), lambda i: (i, 0))
    return pl.pallas_call(
        _outproj_kernel,
        grid=(t // tm,),
        in_specs=[row, pl.BlockSpec((d, d), lambda i: (0, 0)), row, row],
        out_specs=row,
        out_shape=jax.ShapeDtypeStruct((t, d), F32),
        compiler_params=_cparams("arbitrary"),
        name="outproj_residual",
    )(o, w_bf16, x, gate)


def _peer_scores_kernel(x_ref, g_ref, sc_ref, sh_ref, wqt_ref, keys_ref, h_ref, s_ref):
    h = _norm_mod(x_ref[...], g_ref[...], sc_ref[...], sh_ref[...]).astype(BF16)
    h_ref[...] = h
    qt = _dot_nt(wqt_ref[...], h)
    n_hc = s_ref.shape[0]
    half = keys_ref.shape[2]
    for hc in range(n_hc):
        s_ref[hc] = _dot(keys_ref[hc % 2], qt[hc * half:(hc + 1) * half].astype(BF16))


def _peer_scores(x, gain, sc, sh, wqt_bf16, keys_bf16):
    t, d = x.shape
    nq = wqt_bf16.shape[0]
    half = keys_bf16.shape[2]
    n_hc = nq // half
    tm = _pick(t, (256, 128))
    row = pl.BlockSpec((tm, d), lambda i: (i, 0))
    return pl.pallas_call(
        _peer_scores_kernel,
        grid=(t // tm,),
        in_specs=[row, pl.BlockSpec((1, d), lambda i: (0, 0)), row, row,
                  pl.BlockSpec((nq, d), lambda i: (0, 0)),
                  pl.BlockSpec(keys_bf16.shape, lambda i: (0, 0, 0))],
        out_specs=[row, pl.BlockSpec((n_hc, N_KEYS, tm), lambda i: (0, 0, i))],
        out_shape=[jax.ShapeDtypeStruct((t, d), BF16),
                   jax.ShapeDtypeStruct((n_hc, N_KEYS, t), F32)],
        compiler_params=_cparams("arbitrary"),
        name="peer_scores",
    )(x, gain.reshape(1, d), sc, sh, wqt_bf16, keys_bf16)


def _extract_topk(vals, k, break_ties):
    n = vals.shape[0]
    idx = lax.broadcasted_iota(jnp.int32, vals.shape, 0) if break_ties else None
    rank = jnp.full(vals.shape, float(k), F32)
    tops = []
    for r in range(k):
        m = jnp.max(vals, axis=0, keepdims=True)
        hit = vals == m
        if break_ties:
            first = jnp.min(jnp.where(hit, idx, n), axis=0, keepdims=True)
            hit = idx == first
        vals = jnp.where(hit, -jnp.inf, vals)
        rank = jnp.where(hit, float(r), rank)
        tops.append(m)
    return tops, rank


def _peer_select(s1, s2, k, break_ties):
    half = k // 2
    assert half % SUBLANES == 0
    t1, rank1 = _extract_topk(s1, k, break_ties)
    t2, rank2 = _extract_topk(s2, k, break_ties)
    top1 = jnp.concatenate(t1, axis=0)
    top2 = jnp.concatenate(t2, axis=0)
    cand = jnp.concatenate(
        [t1[0] + top2] + [t1[r] + top2[:half] for r in range(1, half)] + [top1[half:] + t2[0]],
        axis=0)
    tops, rank_c = _extract_topk(cand, k, break_ties)
    chosen = jnp.where(rank_c < float(k), 1.0, 0.0)
    counts = [jnp.sum(chosen[:k], axis=0, keepdims=True)]
    for r in range(1, half):
        lo = k + (r - 1) * half
        counts.append(jnp.sum(chosen[lo:lo + half], axis=0, keepdims=True))
    lo = k + (half - 1) * half
    counts += [chosen[lo + r:lo + r + 1] for r in range(half)]
    z = jnp.zeros_like(tops[0])
    for j in range(k):
        z = z + jnp.exp(tops[j] - tops[0])
    lrow = jnp.zeros(s1.shape, F32)
    for r in range(k):
        lrow = jnp.where(rank1 == float(r), counts[r], lrow)
    in1 = rank1 < float(k)
    in2 = rank2 < float(k)
    w1 = jnp.where(in1, jnp.exp(s1 - t1[0]), 0.0)
    w2 = jnp.where(in2, jnp.exp(s2 - t2[0]), 0.0) / z
    n_sel = (jnp.sum(jnp.where(in1, 1.0, 0.0), axis=0, keepdims=True)
             + jnp.sum(jnp.where(in2, 1.0, 0.0), axis=0, keepdims=True)
             + jnp.sum(chosen, axis=0, keepdims=True))
    return rank2, w2, lrow, w1, n_sel


def _peer_topk_kernel(s_ref, r2_ref, w2_ref, lr_ref, w1_ref):
    k = PEER_TOPK
    s1, s2 = s_ref[0], s_ref[1]

    def run(break_ties):
        rank2, w2, lrow, w1, n_sel = _peer_select(s1, s2, k, break_ties)
        r2_ref[0] = rank2.astype(r2_ref.dtype)
        w2_ref[0] = w2.astype(w2_ref.dtype)
        lr_ref[0] = lrow
        w1_ref[0] = w1
        return n_sel

    n_sel = run(False)
    @pl.when(jnp.max(jnp.abs(n_sel - 3.0 * k)) > 0.0)
    def _():
        run(True)


def _peer_topk(s):
    n_hc, n_keys, t = s.shape
    heads = n_hc // 2
    tb = LANES
    out = pl.BlockSpec((1, n_keys, tb), lambda i, h: (h, 0, i))
    return pl.pallas_call(
        _peer_topk_kernel,
        grid=(t // tb, heads),
        in_specs=[pl.BlockSpec((2, n_keys, tb), lambda i, h: (h, 0, i))],
        out_specs=[out] * 4,
        out_shape=[jax.ShapeDtypeStruct((heads, n_keys, t), dt) for dt in (BF16, BF16, F32, F32)],
        compiler_params=_cparams("arbitrary", "arbitrary"),
        name="peer_topk",
    )(s)


def _gelu_tanh(a):
    return 0.5 * a * (1.0 + jnp.tanh(math.sqrt(2.0 / math.pi) * (a + 0.044715 * (a * a * a))))


def _peer_expert_kernel(h_ref, r2_ref, w2_ref, lr_ref, w1_ref, u_ref, vt_ref, x_ref, g_ref,
                        o_ref, acc_sc, act_sc, coef_sc):
    e = pl.program_id(1)
    heads, n_keys, _ = r2_ref.shape
    rows_per_step = u_ref.shape[0] // n_keys

    @pl.when(e == 0)
    def _():
        acc_sc[...] = jnp.zeros_like(acc_sc)

    act_sc[...] = _dot_nt(u_ref[...], h_ref[...])
    for kk in range(rows_per_step):
        i1 = e * rows_per_step + kk
        rows = slice(kk * n_keys, (kk + 1) * n_keys)
        lrows = [lr_ref[hd, pl.ds(i1, 1), :].astype(BF16) for hd in range(heads)]
        w1rows = [w1_ref[hd, pl.ds(i1, 1), :].astype(BF16) for hd in range(heads)]
        for c in range(act_sc.shape[1] // LANES):
            cols = slice(c * LANES, (c + 1) * LANES)
            cmat = None
            for hd in range(heads):
                keep = r2_ref[hd, :, cols] < lrows[hd][:, cols]
                term = jnp.where(keep, w2_ref[hd, :, cols], 0.0) * w1rows[hd][:, cols]
                cmat = term if cmat is None else cmat + term
            coef_sc[rows, cols] = cmat * _gelu_tanh(act_sc[rows, cols]).astype(BF16)
    acc_sc[...] += _dot(vt_ref[...], coef_sc[...])

    @pl.when(e == pl.num_programs(1) - 1)
    def _():
        o_ref[...] = x_ref[...] + g_ref[...] * jnp.transpose(acc_sc[...])


def _peer_expert(h, r2, w2, lr, w1, u_bf16, vt_bf16, x, gate):
    t, d = x.shape
    heads, n_keys, _ = r2.shape
    n_exp = u_bf16.shape[0]
    tb = _pick(t, (512, 256, 128))
    eb = 512
    row = pl.BlockSpec((tb, d), lambda i, e: (i, 0))
    sel = pl.BlockSpec((heads, n_keys, tb), lambda i, e: (0, 0, i))
    return pl.pallas_call(
        _peer_expert_kernel,
        grid=(t // tb, n_exp // eb),
        in_specs=[row, sel, sel, sel, sel,
                  pl.BlockSpec((eb, d), lambda i, e: (e, 0)),
                  pl.BlockSpec((d, eb), lambda i, e: (0, e)),
                  row, row],
        out_specs=row,
        out_shape=jax.ShapeDtypeStruct((t, d), F32),
        scratch_shapes=[pltpu.VMEM((d, tb), F32), pltpu.VMEM((eb, tb), F32),
                        pltpu.VMEM((eb, tb), BF16)],
        compiler_params=_cparams("arbitrary", "arbitrary"),
        name="peer_experts",
    )(h, r2, w2, lr, w1, u_bf16, vt_bf16, x, gate)


def _peer(x, gain, sc, sh, gate, w_query, sub_keys, expert_u, expert_v):
    h, s = _peer_scores(x, gain, sc, sh, jnp.transpose(w_query).astype(BF16),
                        sub_keys.astype(BF16))
    r2, w2, lr, w1 = _peer_topk(s)
    return _peer_expert(h, r2, w2, lr, w1, expert_u.astype(BF16),
                        jnp.transpose(expert_v).astype(BF16), x, gate)


def kernel(x_prompt, x_sample, cache_k_sb, cache_v_sb, cache_k_diff, cache_v_diff, page_table, c_prompt, c_sample, w_ada, b_ada, norm_mix, norm_ffn, w_qkv_sb, w_o_sb, w_qkv_diff, w_o_diff, diff_q_gain, diff_k_gain, diff_lambda_q1, diff_lambda_k1, diff_lambda_q2, diff_lambda_k2, diff_sub_gain, peer_w_query, peer_sub_keys, peer_u, peer_v):
    batch, seq, d = x_prompt.shape
    n_seq, n_new, _ = x_sample.shape
    assert d == SB_HEADS * HEAD_DIM == DIFF_HEADS * 2 * HEAD_DIM and n_new == SUBLANES
    depth = w_ada.shape[0]
    tp, ts = batch * seq, n_seq * n_new
    past_len = page_table.shape[1] * cache_k_sb.shape[2]

    x = jnp.concatenate([x_prompt.reshape(tp, d), x_sample.reshape(ts, d)], axis=0)
    c_all = jnp.concatenate([c_prompt, c_sample], axis=0)
    n_c = c_all.shape[0]
    c_all = jnp.pad(c_all, ((0, (-n_c) % SUBLANES), (0, 0)))
    mod = _ada(c_all, w_ada, b_ada)

    def per_token(layer, j):
        m = mod[layer, :, j * d:(j + 1) * d]
        return jnp.concatenate([jnp.repeat(m[:batch], seq, axis=0),
                                jnp.repeat(m[batch:n_c], n_new, axis=0)], axis=0)

    pos = jnp.concatenate([jnp.tile(jnp.arange(seq), batch),
                           jnp.tile(past_len + jnp.arange(n_new), n_seq)])
    cos_t, s1_t, s2_t = _rope_tables(pos)

    sb_k, sb_v, df_k, df_v = [], [], [], []
    for i in range(depth):
        sh_m, sc_m, g_m, sh_f, sc_f, g_f = [per_token(i, j) for j in range(6)]
        j = i // 2
        if i % 2 == 0:
            q, k, v = _qkv(x, norm_mix[i], sc_m, sh_m, w_qkv_sb[j].astype(BF16))
            o_p = _sb_prompt(q, k, v, batch, seq)
            o_s = _decode_attention("sb", q, k, v, cache_k_sb, cache_v_sb, j, page_table, tp,
                                    scale=HEAD_DIM ** -0.5)
            w_o = w_o_sb[j]
            sb_k.append(k)
            sb_v.append(v)
        else:
            lambda_init = 0.8 - 0.6 * math.exp(-0.3 * i)
            q, k, v = _qkv(x, norm_mix[i], sc_m, sh_m, w_qkv_diff[j].astype(BF16))
            q, k = _qknorm_rope(q, k, diff_q_gain[j], diff_k_gain[j], cos_t, s1_t, s2_t)
            lam_vecs = (diff_lambda_q1[j], diff_lambda_k1[j], diff_lambda_q2[j], diff_lambda_k2[j])
            o_p = _diff_prompt(q, k, v, lam_vecs, diff_sub_gain[j], lambda_init, batch, seq)
            o_s = _decode_attention(
                "diff", q, k, v, cache_k_diff, cache_v_diff, j, page_table, tp,
                extra=tuple(a.reshape(1, HEAD_DIM) for a in lam_vecs)
                + (diff_sub_gain[j].reshape(1, LANES),),
                scale=HEAD_DIM ** -0.5, lambda_init=lambda_init)
            w_o = w_o_diff[j]
            df_k.append(k)
            df_v.append(v)
        x = _outproj(jnp.concatenate([o_p, o_s], axis=0), w_o.astype(BF16), x, g_m)
        x = _peer(x, norm_ffn[i], sc_f, sh_f, g_f, peer_w_query[i], peer_sub_keys[i],
                  peer_u[i], peer_v[i])

    def rows(stack, lo, hi, shape):
        return jnp.stack([a[lo:hi].reshape(shape) for a in stack])

    t = tp + ts
    return (
        x[:tp].reshape(batch, seq, d),
        x[tp:].reshape(n_seq, n_new, d),
        rows(sb_k, 0, tp, (batch, seq, SB_HEADS, HEAD_DIM)),
        rows(sb_v, 0, tp, (batch, seq, SB_HEADS, HEAD_DIM)),
        rows(df_k, 0, tp, (batch, seq, DIFF_HEADS, 2, HEAD_DIM)),
        rows(df_v, 0, tp, (batch, seq, DIFF_HEADS, 2 * HEAD_DIM)),
        rows(sb_k, tp, t, (n_seq, n_new, SB_HEADS, HEAD_DIM)),
        rows(sb_v, tp, t, (n_seq, n_new, SB_HEADS, HEAD_DIM)),
        rows(df_k, tp, t, (n_seq, n_new, DIFF_HEADS, 2, HEAD_DIM)),
        rows(df_v, tp, t, (n_seq, n_new, DIFF_HEADS, 2 * HEAD_DIM)),
    )
```

```python
import functools
import math

import jax
import jax.numpy as jnp
from jax import lax
from jax.experimental import pallas as pl
from jax.experimental.pallas import tpu as pltpu

F32 = jnp.float32
BF16 = jnp.bfloat16

RMS_EPS = 1e-6
ROPE_THETA = 500000.0
SB_HEADS = 16
DIFF_HEADS = 8
HEAD_DIM = 64
ROPE_DIM = HEAD_DIM // 4
N_KEYS = 128
PEER_HEADS = 8
PEER_TOPK = 16
LANES = 128
SUBLANES = 8
VMEM_LIMIT = 56 * 1024 * 1024
EXP_ZERO_BELOW = -105.0


def _cparams(*sem):
    return pltpu.CompilerParams(dimension_semantics=sem, vmem_limit_bytes=VMEM_LIMIT)


def _pick(n, choices):
    for c in choices:
        if n % c == 0:
            return c
    raise ValueError(f"no block size in {choices} divides {n}")


def _dot(a, b):
    return jnp.dot(a, b, preferred_element_type=F32)


def _dot_nt(a, b):
    return lax.dot_general(a, b, (((1,), (1,)), ((), ())), preferred_element_type=F32)


def _split3(a):
    hi = a.astype(BF16)
    r = a - hi.astype(F32)
    mid = r.astype(BF16)
    lo = (r - mid.astype(F32)).astype(BF16)
    return hi, mid, lo


def _dot_f32_lhs(a, m):
    hi, mid, lo = _split3(a)
    return _dot(hi, m) + _dot(mid, m) + _dot(lo, m)


def _dot_f32_rhs(m, a):
    hi, mid, lo = _split3(a)
    return _dot(m, hi) + _dot(m, mid) + _dot(m, lo)


def _softplus(z):
    return jnp.maximum(z, 0.0) + jnp.log(1.0 + jnp.exp(-jnp.abs(z)))


def _expand_rows(g_ref, n=SUBLANES):
    r, d = g_ref.shape
    return jnp.concatenate([jnp.broadcast_to(g_ref[i:i + 1, :], (n, d)) for i in range(r)], axis=0)


def _norm_mod(x, gain, sc, sh):
    ms = jnp.mean(x * x, axis=-1, keepdims=True)
    y = x * lax.rsqrt(ms + RMS_EPS)
    return (y * gain) * (1.0 + sc) + sh


def _ada_kernel(c_ref, w_ref, b_ref, o_ref):
    c = c_ref[...]
    s = c / (1.0 + jnp.exp(-c))
    o_ref[0] = _dot(s.astype(BF16), w_ref[0].astype(BF16)) + b_ref[0]


def _ada(c_all, w_ada, b_ada):
    n_layers, d, n = w_ada.shape
    r = c_all.shape[0]
    tn = d
    return pl.pallas_call(
        _ada_kernel,
        grid=(n_layers, n // tn),
        in_specs=[
            pl.BlockSpec((r, d), lambda l, j: (0, 0)),
            pl.BlockSpec((1, d, tn), lambda l, j: (l, 0, j)),
            pl.BlockSpec((1, 1, tn), lambda l, j: (l, 0, j)),
        ],
        out_specs=pl.BlockSpec((1, r, tn), lambda l, j: (l, 0, j)),
        out_shape=jax.ShapeDtypeStruct((n_layers, r, n), F32),
        compiler_params=_cparams("arbitrary", "arbitrary"),
        name="ada_modulation",
    )(c_all, w_ada, b_ada.reshape(n_layers, 1, n))


def _qkv_kernel(x_ref, g_ref, sc_ref, sh_ref, w_ref, q_ref, k_ref, v_ref):
    d = x_ref.shape[1]
    h = _norm_mod(x_ref[...], g_ref[...], _expand_rows(sc_ref), _expand_rows(sh_ref)).astype(BF16)
    r = _dot(h, w_ref[...])
    q_ref[...] = r[:, :d]
    k_ref[...] = r[:, d:2 * d]
    v_ref[...] = r[:, 2 * d:]


def _qkv(x, gain, sc, sh, w_bf16):
    t, d = x.shape
    tm = _pick(t, (256, 128))
    row = pl.BlockSpec((tm, d), lambda i: (i, 0))
    grp = pl.BlockSpec((tm // SUBLANES, d), lambda i: (i, 0))
    return pl.pallas_call(
        _qkv_kernel,
        grid=(t // tm,),
        in_specs=[row, pl.BlockSpec((1, d), lambda i: (0, 0)), grp, grp,
                  pl.BlockSpec((d, 3 * d), lambda i: (0, 0))],
        out_specs=[row, row, row],
        out_shape=[jax.ShapeDtypeStruct((t, d), F32)] * 3,
        compiler_params=_cparams("arbitrary"),
        name="norm_qkv",
    )(x, gain.reshape(1, d), sc, sh, w_bf16)


def _qknorm_rope_kernel(q_ref, k_ref, qg_ref, kg_ref, c_ref, s1_ref, s2_ref, qo_ref, ko_ref):
    d = q_ref.shape[1]
    r = lax.broadcasted_iota(jnp.int32, (LANES, LANES), 0) // HEAD_DIM
    c = lax.broadcasted_iota(jnp.int32, (LANES, LANES), 1) // HEAD_DIM
    seg = jnp.where(r == c, 1.0, 0.0).astype(BF16)
    cosv, s1, s2 = c_ref[...], s1_ref[...], s2_ref[...]
    for src, gain, dst in ((q_ref, qg_ref, qo_ref), (k_ref, kg_ref, ko_ref)):
        g = gain[...]
        for j in range(d // LANES):
            x = src[:, j * LANES:(j + 1) * LANES]
            ms = _dot_f32_lhs(x * x, seg) * (1.0 / HEAD_DIM)
            y = x * lax.rsqrt(ms + RMS_EPS) * g
            up = pltpu.roll(y, LANES - ROPE_DIM // 2, 1)
            dn = pltpu.roll(y, ROPE_DIM // 2, 1)
            dst[:, j * LANES:(j + 1) * LANES] = y * cosv + up * s1 + dn * s2


def _qknorm_rope(q, k, q_gain, k_gain, cos_t, s1_t, s2_t):
    t, d = q.shape
    tm = _pick(t, (512, 256, 128))
    row = pl.BlockSpec((tm, d), lambda i: (i, 0))
    tab = pl.BlockSpec((tm, LANES), lambda i: (i, 0))
    gs = pl.BlockSpec((1, LANES), lambda i: (0, 0))
    tile = LANES // HEAD_DIM
    return pl.pallas_call(
        _qknorm_rope_kernel,
        grid=(t // tm,),
        in_specs=[row, row, gs, gs, tab, tab, tab],
        out_specs=[row, row],
        out_shape=[jax.ShapeDtypeStruct((t, d), F32)] * 2,
        compiler_params=_cparams("arbitrary"),
        name="qknorm_rope",
    )(q, k, jnp.tile(q_gain, tile).reshape(1, LANES), jnp.tile(k_gain, tile).reshape(1, LANES),
      cos_t, s1_t, s2_t)


def _rope_tables(pos):
    half = ROPE_DIM // 2
    inv = ROPE_THETA ** (-jnp.arange(half, dtype=F32) / half)
    ang = pos.astype(F32)[:, None] * inv[None, :]
    cos, sin = jnp.cos(ang), jnp.sin(ang)
    t = pos.shape[0]
    pad = jnp.zeros((t, HEAD_DIM - ROPE_DIM), F32)
    z = jnp.zeros((t, half), F32)
    c = jnp.concatenate([cos, cos, pad + 1.0], axis=1)
    s1 = jnp.concatenate([-sin, z, pad], axis=1)
    s2 = jnp.concatenate([z, sin, pad], axis=1)
    rep = LANES // HEAD_DIM
    return jnp.tile(c, (1, rep)), jnp.tile(s1, (1, rep)), jnp.tile(s2, (1, rep))


def _sb_prompt_kernel(q_ref, k_ref, v_ref, o_ref, *, tq, tk, scale):
    i = pl.program_id(2)
    q0 = i * tq
    jr = lax.broadcasted_iota(jnp.int32, (tk, tk), 0)
    jc = lax.broadcasted_iota(jnp.int32, (tk, tk), 1)
    later = jnp.where(jr > jc, 1.0, 0.0).astype(BF16)
    row = q0 + lax.broadcasted_iota(jnp.int32, (tq, 1), 0)
    n_kb = (q0 + tq) // tk
    n_heads = LANES // HEAD_DIM
    qs = [(q_ref[:, hh * HEAD_DIM:(hh + 1) * HEAD_DIM] * scale).astype(BF16)
          for hh in range(n_heads)]

    def cond(carry):
        kb, state = carry
        live = functools.reduce(jnp.maximum, [jnp.max(cs) for _, cs in state])
        return jnp.logical_and(kb >= 0, live > EXP_ZERO_BELOW)

    def body(carry):
        kb, state = carry
        k0 = pl.multiple_of(kb * tk, tk)
        kblk = k_ref[pl.ds(k0, tk), :].astype(BF16)
        vblk = v_ref[pl.ds(k0, tk), :].astype(BF16)
        col = k0 + lax.broadcasted_iota(jnp.int32, (1, tk), 1)
        causal = col < row
        new = []
        for hh, (acc, cs) in enumerate(state):
            lo, hi = hh * HEAD_DIM, (hh + 1) * HEAD_DIM
            z = _dot_nt(qs[hh], kblk[:, lo:hi])
            sp = _softplus(z)
            log_keep = jnp.where(causal, -sp, 0.0)
            after = _dot_f32_lhs(log_keep, later) + cs
            w = jnp.where(causal, jnp.exp((z - sp) + after), 0.0)
            acc = acc + _dot(w.astype(BF16), vblk[:, lo:hi])
            cs = cs + jnp.sum(log_keep, axis=1, keepdims=True)
            new.append((acc, cs))
        return kb - 1, tuple(new)

    init = tuple((jnp.zeros((tq, HEAD_DIM), F32), jnp.zeros((tq, 1), F32))
                 for _ in range(n_heads))
    _, state = lax.while_loop(cond, body, (n_kb - 1, init))
    o_ref[...] = jnp.concatenate([acc for acc, _ in state], axis=1)


def _sb_prompt(q, k, v, batch, seq):
    d = q.shape[1]
    tq = _pick(seq, (256, 128))
    tk = tq
    nq = seq // tq
    return pl.pallas_call(
        functools.partial(_sb_prompt_kernel, tq=tq, tk=tk, scale=HEAD_DIM ** -0.5),
        grid=(batch, d // LANES, nq),
        in_specs=[
            pl.BlockSpec((tq, LANES), lambda b, h, i: (b * nq + i, h)),
            pl.BlockSpec((seq, LANES), lambda b, h, i: (b, h)),
            pl.BlockSpec((seq, LANES), lambda b, h, i: (b, h)),
        ],
        out_specs=pl.BlockSpec((tq, LANES), lambda b, h, i: (b * nq + i, h)),
        out_shape=jax.ShapeDtypeStruct((batch * seq, d), F32),
        compiler_params=_cparams("arbitrary", "arbitrary", "arbitrary"),
        name="sb_prompt_attention",
    )(q, k, v)


def _lambda(lq1_ref, lk1_ref, lq2_ref, lk2_ref, lambda_init):
    a = jnp.sum(lq1_ref[...] * lk1_ref[...], axis=1, keepdims=True)
    b = jnp.sum(lq2_ref[...] * lk2_ref[...], axis=1, keepdims=True)
    return jnp.exp(a) - jnp.exp(b) + lambda_init


def _diff_prompt_kernel(q_ref, k_ref, v_ref, lq1_ref, lk1_ref, lq2_ref, lk2_ref, sg_ref, o_ref,
                        *, tq, scale, lambda_init):
    i = pl.program_id(2)
    q0 = i * tq
    tk = tq
    lam = _lambda(lq1_ref, lk1_ref, lq2_ref, lk2_ref, lambda_init)
    qs = [(q_ref[:, c * HEAD_DIM:(c + 1) * HEAD_DIM] * scale).astype(BF16) for c in range(2)]
    row = lax.broadcasted_iota(jnp.int32, (tq, 1), 0)
    col = lax.broadcasted_iota(jnp.int32, (1, tk), 1)
    diag_mask = col <= row

    def step(kb, carry, masked):
        k0 = pl.multiple_of(kb * tk, tk)
        vblk = v_ref[pl.ds(k0, tk), :].astype(BF16)
        vext = jnp.concatenate([vblk, jnp.ones_like(vblk)], axis=1)
        new = []
        for c in range(2):
            m, acc = carry[c]
            kblk = k_ref[pl.ds(k0, tk), c * HEAD_DIM:(c + 1) * HEAD_DIM].astype(BF16)
            s = _dot_nt(qs[c], kblk)
            if masked:
                s = jnp.where(diag_mask, s, -jnp.inf)
            m_new = jnp.maximum(m, jnp.max(s, axis=1, keepdims=True))
            p = jnp.exp(s - m_new)
            acc = acc * jnp.exp(m - m_new) + _dot(p.astype(BF16), vext)
            new.append((m_new, acc))
        return tuple(new)

    init = tuple((jnp.full((tq, 1), -jnp.inf, F32), jnp.zeros((tq, 2 * LANES), F32))
                 for _ in range(2))
    carry = lax.fori_loop(0, i, lambda kb, cr: step(kb, cr, False), init)
    (_, a0), (_, a1) = step(i, carry, True)
    o = a0[:, :LANES] / a0[:, LANES:] - lam * (a1[:, :LANES] / a1[:, LANES:])
    ms = jnp.mean(o * o, axis=-1, keepdims=True)
    o_ref[...] = (o * lax.rsqrt(ms + RMS_EPS) * sg_ref[...]) * (1.0 - lambda_init)


def _diff_prompt(q, k, v, lam_vecs, sub_gain, lambda_init, batch, seq):
    d = q.shape[1]
    tq = _pick(seq, (512, 256, 128))
    nq = seq // tq
    vec = pl.BlockSpec((1, HEAD_DIM), lambda b, h, i: (0, 0))
    return pl.pallas_call(
        functools.partial(_diff_prompt_kernel, tq=tq, scale=HEAD_DIM ** -0.5,
                          lambda_init=lambda_init),
        grid=(batch, d // LANES, nq),
        in_specs=[
            pl.BlockSpec((tq, LANES), lambda b, h, i: (b * nq + i, h)),
            pl.BlockSpec((seq, LANES), lambda b, h, i: (b, h)),
            pl.BlockSpec((seq, LANES), lambda b, h, i: (b, h)),
            vec, vec, vec, vec,
            pl.BlockSpec((1, LANES), lambda b, h, i: (0, 0)),
        ],
        out_specs=pl.BlockSpec((tq, LANES), lambda b, h, i: (b * nq + i, h)),
        out_shape=jax.ShapeDtypeStruct((batch * seq, d), F32),
        compiler_params=_cparams("arbitrary", "arbitrary", "arbitrary"),
        name="diff_prompt_attention",
    )(q, k, v, *[x.reshape(1, HEAD_DIM) for x in lam_vecs], sub_gain.reshape(1, LANES))


def _block_diag_queries(q, scale):
    n_q, d = q.shape
    groups = d // HEAD_DIM
    rep = jnp.concatenate([q * scale] * groups, axis=0)
    rg = lax.broadcasted_iota(jnp.int32, rep.shape, 0) // n_q
    cg = lax.broadcasted_iota(jnp.int32, rep.shape, 1) // HEAD_DIM
    return jnp.where(rg == cg, rep, 0.0).astype(BF16)


def _pad_rows(x, rows):
    return jnp.concatenate([x, jnp.zeros((rows - x.shape[0], x.shape[1]), x.dtype)], axis=0)


def _take_group_blocks(acc, n_q, width):
    rows, d = acc.shape
    a3 = acc.reshape(rows // n_q, n_q, d)
    g = lax.broadcasted_iota(jnp.int32, a3.shape, 0)
    cg = lax.broadcasted_iota(jnp.int32, a3.shape, 2) // width
    return jnp.sum(jnp.where(g == cg, a3, 0.0), axis=0)


def _load_rows_page(ref, page):
    groups = ref.shape[0] // page
    pieces = [ref[pl.ds(g, page, stride=groups), :] for g in range(groups)]
    return jnp.concatenate(pieces, axis=1)


def _new_rows_block(qbd, kn_ref, vn_ref, page, strict):
    n_q = kn_ref.shape[0]
    k_pad = _pad_rows(kn_ref[...], page).astype(BF16)
    v_pad = _pad_rows(vn_ref[...], page).astype(BF16)
    qi = lax.broadcasted_iota(jnp.int32, (LANES, page), 0) % n_q
    key = lax.broadcasted_iota(jnp.int32, (LANES, page), 1)
    valid = key < qi if strict else key <= qi
    return _dot_nt(qbd, k_pad), lambda w: _dot(w.astype(BF16), v_pad), valid


def _page_scores(qbd, kt_ref):
    return _dot(qbd, kt_ref[...].astype(BF16))


def _sb_blocks(blocks, cs):
    n = blocks[0][0].shape[1]
    jr = lax.broadcasted_iota(jnp.int32, (n, n), 0)
    jc = lax.broadcasted_iota(jnp.int32, (n, n), 1)
    later = jnp.where(jr > jc, 1.0, 0.0).astype(BF16)
    out = None
    for z, values, valid in blocks:
        sp = _softplus(z)
        log_keep = -sp if valid is None else jnp.where(valid, -sp, 0.0)
        after = _dot_f32_lhs(log_keep, later) + cs
        w = jnp.exp((z - sp) + after)
        if valid is not None:
            w = jnp.where(valid, w, 0.0)
        c = values(w)
        out = c if out is None else out + c
        cs = cs + jnp.sum(log_keep, axis=1, keepdims=True)
    return out, cs


def _sb_decode_kernel(pt_ref, q_ref, kn_ref, vn_ref, *rest, scale, pps, page):
    kp, vp, o_ref = rest[:pps], rest[pps:2 * pps], rest[2 * pps]
    qbd_sc, acc_sc, cs_sc = rest[2 * pps + 1:]
    s = pl.program_id(1)
    n_q = q_ref.shape[0]

    @pl.when(s == 0)
    def _():
        qbd = _block_diag_queries(q_ref[...], scale)
        qbd_sc[...] = qbd
        new = _new_rows_block(qbd, kn_ref, vn_ref, page, True)
        acc_sc[...], cs_sc[...] = _sb_blocks([new], jnp.zeros(cs_sc.shape, F32))

    @pl.when(jnp.max(cs_sc[...]) > EXP_ZERO_BELOW)
    def _():
        qbd = qbd_sc[...]
        blocks = [(_page_scores(qbd, kp[i]),
                   lambda w, i=i: _dot_nt(w.astype(BF16), vp[i][...].astype(BF16)), None)
                  for i in range(pps)]
        out, cs = _sb_blocks(blocks, cs_sc[...])
        acc_sc[...] += out
        cs_sc[...] = cs

    @pl.when(s == pl.num_programs(1) - 1)
    def _():
        o_ref[...] = _take_group_blocks(acc_sc[...], n_q, HEAD_DIM)


def _diff_blocks(blocks, m_old, l_old, acc_old):
    zs = [z if valid is None else jnp.where(valid, z, -jnp.inf) for z, _, valid in blocks]
    m_new = functools.reduce(jnp.maximum, [jnp.max(z, axis=1, keepdims=True) for z in zs])
    if acc_old is None:
        l, acc = jnp.zeros_like(m_new), None
    else:
        m_new = jnp.maximum(m_old, m_new)
        alpha = jnp.exp(m_old - m_new)
        l, acc = l_old * alpha, acc_old * alpha
    for (_, values, _), z in zip(blocks, zs):
        p = jnp.exp(z - m_new)
        l = l + jnp.sum(p, axis=1, keepdims=True)
        c = values(p)
        acc = c if acc is None else acc + c
    return m_new, l, acc


def _diff_decode_kernel(pt_ref, q_ref, kn_ref, vn_ref, *rest, scale, lambda_init, pps, page):
    kp, vp = rest[:pps], rest[pps:2 * pps]
    lq1_ref, lk1_ref, lq2_ref, lk2_ref, sg_ref, o_ref = rest[2 * pps:2 * pps + 6]
    qbd_sc, acc_sc, m_sc, l_sc = rest[2 * pps + 6:]
    s = pl.program_id(1)
    n_q = q_ref.shape[0]

    @pl.when(s == 0)
    def _():
        qbd = _block_diag_queries(q_ref[...], scale)
        qbd_sc[...] = qbd
        new = _new_rows_block(qbd, kn_ref, vn_ref, page, False)
        m_sc[...], l_sc[...], acc_sc[...] = _diff_blocks([new], None, None, None)

    qbd = qbd_sc[...]
    blocks = [(_page_scores(qbd, kp[i]),
               lambda w, i=i: _dot(w.astype(BF16), _load_rows_page(vp[i], page).astype(BF16)), None)
              for i in range(pps)]
    m_sc[...], l_sc[...], acc_sc[...] = _diff_blocks(blocks, m_sc[...], l_sc[...], acc_sc[...])

    @pl.when(s == pl.num_programs(1) - 1)
    def _():
        lam = _lambda(lq1_ref, lk1_ref, lq2_ref, lk2_ref, lambda_init)
        d = acc_sc.shape[1]
        full = acc_sc[...] * (1.0 / l_sc[...])
        a4 = full.reshape(DIFF_HEADS, 2, n_q, d)
        hd = lax.broadcasted_iota(jnp.int32, (DIFF_HEADS, n_q, d), 0)
        cg = lax.broadcasted_iota(jnp.int32, (DIFF_HEADS, n_q, d), 2) // LANES
        o0 = jnp.sum(jnp.where(hd == cg, a4[:, 0], 0.0), axis=0)
        o1 = jnp.sum(jnp.where(hd == cg, a4[:, 1], 0.0), axis=0)
        o = o0 - lam * o1
        sg = sg_ref[...]
        for j in range(d // LANES):
            blk = o[:, j * LANES:(j + 1) * LANES]
            ms = jnp.mean(blk * blk, axis=-1, keepdims=True)
            o_ref[:, j * LANES:(j + 1) * LANES] = (
                (blk * lax.rsqrt(ms + RMS_EPS) * sg) * (1.0 - lambda_init))


def _transposed_pages(cache):
    t = jnp.transpose(cache, (0, 1) + tuple(range(3, cache.ndim)) + (2,))
    return t.reshape(t.shape[0], t.shape[1], -1, t.shape[-1])


def _decode_attention(kind, q, k, v, cache_k, cache_v, layer, page_table, row0, extra=(), **kw):
    d = q.shape[1]
    n_seq, n_pages = page_table.shape
    n_layers, n_phys, page = cache_k.shape[:3]
    n_q = SUBLANES
    blk0 = row0 // n_q
    pps = _pick(n_pages, (4, 2, 1))
    new = pl.BlockSpec((n_q, d), lambda b, s, pt: (blk0 + b, 0))

    def page_specs(cache):
        return [pl.BlockSpec((None, None) + cache.shape[2:],
                             lambda b, s, pt, i=i: (layer, pt[b, n_pages - 1 - (s * pps + i)], 0, 0))
                for i in range(pps)]

    cache_k = _transposed_pages(cache_k)
    kw = dict(kw, pps=pps, page=page)
    col = pltpu.VMEM((LANES, 1), F32)
    if kind == "sb":
        cache_v = _transposed_pages(cache_v)
        body = functools.partial(_sb_decode_kernel, **kw)
        extra_specs = []
        scratch = [pltpu.VMEM((LANES, d), BF16), pltpu.VMEM((LANES, d), F32), col]
    else:
        cache_v = cache_v.reshape(n_layers, n_phys, -1, cache_v.shape[-1])
        body = functools.partial(_diff_decode_kernel, **kw)
        vec = pl.BlockSpec((1, HEAD_DIM), lambda b, s, pt: (0, 0))
        extra_specs = [vec, vec, vec, vec, pl.BlockSpec((1, LANES), lambda b, s, pt: (0, 0))]
        scratch = [pltpu.VMEM((LANES, d), BF16), pltpu.VMEM((LANES, d), F32), col, col]
    return pl.pallas_call(
        body,
        grid_spec=pltpu.PrefetchScalarGridSpec(
            num_scalar_prefetch=1,
            grid=(n_seq, n_pages // pps),
            in_specs=[new, new, new] + page_specs(cache_k) + page_specs(cache_v) + extra_specs,
            out_specs=pl.BlockSpec((n_q, d), lambda b, s, pt: (b, 0)),
            scratch_shapes=scratch,
        ),
        out_shape=jax.ShapeDtypeStruct((n_seq * n_q, d), F32),
        compiler_params=_cparams("arbitrary", "arbitrary"),
        name=kind + "_decode_attention",
    )(page_table, q, k, v, *([cache_k] * pps), *([cache_v] * pps), *extra)


def _outproj_kernel(o_ref, w_ref, x_ref, g_ref, y_ref):
    y_ref[...] = x_ref[...] + _expand_rows(g_ref) * _dot(o_ref[...].astype(BF16), w_ref[...])


def _outproj(o, w_bf16, x, gate):
    t, d = x.shape
    tm = _pick(t, (512, 256, 128))
    row = pl.BlockSpec((tm, d), lambda i: (i, 0))
    return pl.pallas_call(
        _outproj_kernel,
        grid=(t // tm,),
        in_specs=[row, pl.BlockSpec((d, d), lambda i: (0, 0)), row,
                  pl.BlockSpec((tm // SUBLANES, d), lambda i: (i, 0))],
        out_specs=row,
        out_shape=jax.ShapeDtypeStruct((t, d), F32),
        compiler_params=_cparams("arbitrary"),
        name="outproj_residual",
    )(o, w_bf16, x, gate)


def _peer_scores_kernel(x_ref, g_ref, sc_ref, sh_ref, wqt_ref, keys_ref, h_ref, s_ref):
    h = _norm_mod(x_ref[...], g_ref[...], _expand_rows(sc_ref), _expand_rows(sh_ref)).astype(BF16)
    h_ref[...] = h
    qt = _dot_nt(wqt_ref[...], h)
    n_hc = s_ref.shape[0]
    half = keys_ref.shape[2]
    for hc in range(n_hc):
        s_ref[hc] = _dot(keys_ref[hc % 2], qt[hc * half:(hc + 1) * half].astype(BF16))


def _peer_scores(x, gain, sc, sh, wqt_bf16, keys_bf16):
    t, d = x.shape
    nq = wqt_bf16.shape[0]
    half = keys_bf16.shape[2]
    n_hc = nq // half
    tm = _pick(t, (256, 128))
    row = pl.BlockSpec((tm, d), lambda i: (i, 0))
    return pl.pallas_call(
        _peer_scores_kernel,
        grid=(t // tm,),
        in_specs=[row, pl.BlockSpec((1, d), lambda i: (0, 0)),
                  pl.BlockSpec((tm // SUBLANES, d), lambda i: (i, 0)),
                  pl.BlockSpec((tm // SUBLANES, d), lambda i: (i, 0)),
                  pl.BlockSpec((nq, d), lambda i: (0, 0)),
                  pl.BlockSpec(keys_bf16.shape, lambda i: (0, 0, 0))],
        out_specs=[row, pl.BlockSpec((n_hc, N_KEYS, tm), lambda i: (0, 0, i))],
        out_shape=[jax.ShapeDtypeStruct((t, d), BF16),
                   jax.ShapeDtypeStruct((n_hc, N_KEYS, t), F32)],
        compiler_params=_cparams("arbitrary"),
        name="peer_scores",
    )(x, gain.reshape(1, d), sc, sh, wqt_bf16, keys_bf16)


def _extract_topk(vals, k, break_ties):
    n = vals.shape[0]
    idx = lax.broadcasted_iota(jnp.int32, vals.shape, 0) if break_ties else None
    rank = jnp.full(vals.shape, float(k), F32)
    tops = []
    for r in range(k):
        m = jnp.max(vals, axis=0, keepdims=True)
        hit = vals == m
        if break_ties:
            first = jnp.min(jnp.where(hit, idx, n), axis=0, keepdims=True)
            hit = idx == first
        vals = jnp.where(hit, -jnp.inf, vals)
        rank = jnp.where(hit, float(r), rank)
        tops.append(m)
    return tops, rank


def _peer_select(s1, s2, k, break_ties):
    half = k // 2
    assert half % SUBLANES == 0
    t1, rank1 = _extract_topk(s1, k, break_ties)
    t2, rank2 = _extract_topk(s2, k, break_ties)
    top1 = jnp.concatenate(t1, axis=0)
    top2 = jnp.concatenate(t2, axis=0)
    cand = jnp.concatenate(
        [t1[0] + top2] + [t1[r] + top2[:half] for r in range(1, half)] + [top1[half:] + t2[0]],
        axis=0)
    tops, rank_c = _extract_topk(cand, k, break_ties)
    chosen = jnp.where(rank_c < float(k), 1.0, 0.0)
    counts = [jnp.sum(chosen[:k], axis=0, keepdims=True)]
    for r in range(1, half):
        lo = k + (r - 1) * half
        counts.append(jnp.sum(chosen[lo:lo + half], axis=0, keepdims=True))
    lo = k + (half - 1) * half
    counts += [chosen[lo + r:lo + r + 1] for r in range(half)]
    z = jnp.zeros_like(tops[0])
    for j in range(k):
        z = z + jnp.exp(tops[j] - tops[0])
    lrow = jnp.zeros(s1.shape, F32)
    for r in range(k):
        lrow = jnp.where(rank1 == float(r), counts[r], lrow)
    in1 = rank1 < float(k)
    in2 = rank2 < float(k)
    w1 = jnp.where(in1, jnp.exp(s1 - t1[0]), 0.0)
    w2 = jnp.where(in2, jnp.exp(s2 - t2[0]), 0.0) / z
    n_sel = (jnp.sum(jnp.where(in1, 1.0, 0.0), axis=0, keepdims=True)
             + jnp.sum(jnp.where(in2, 1.0, 0.0), axis=0, keepdims=True)
             + jnp.sum(chosen, axis=0, keepdims=True))
    return rank2, w2, lrow, w1, n_sel


def _peer_topk_kernel(s_ref, r2_ref, w2_ref, lr_ref, w1_ref):
    k = PEER_TOPK
    heads = r2_ref.shape[0]

    def run(break_ties):
        miss = None
        for hd in range(heads):
            s1, s2 = s_ref[2 * hd], s_ref[2 * hd + 1]
            rank2, w2, lrow, w1, n_sel = _peer_select(s1, s2, k, break_ties)
            r2_ref[hd] = rank2.astype(r2_ref.dtype)
            w2_ref[hd] = w2.astype(w2_ref.dtype)
            lr_ref[hd] = lrow
            w1_ref[hd] = w1
            d = jnp.abs(n_sel - 3.0 * k)
            miss = d if miss is None else miss + d
        return miss

    miss = run(False)
    @pl.when(jnp.max(miss) > 0.0)
    def _():
        run(True)


def _peer_topk(s):
    n_hc, n_keys, t = s.shape
    heads = n_hc // 2
    tb = LANES
    hps = 2
    out = pl.BlockSpec((hps, n_keys, tb), lambda i, h: (h, 0, i))
    return pl.pallas_call(
        _peer_topk_kernel,
        grid=(t // tb, heads // hps),
        in_specs=[pl.BlockSpec((2 * hps, n_keys, tb), lambda i, h: (h, 0, i))],
        out_specs=[out] * 4,
        out_shape=[jax.ShapeDtypeStruct((heads, n_keys, t), dt) for dt in (BF16, BF16, F32, F32)],
        compiler_params=_cparams("arbitrary", "arbitrary"),
        name="peer_topk",
    )(s)


def _gelu_tanh(a):
    return 0.5 * a * (1.0 + jnp.tanh(math.sqrt(2.0 / math.pi) * (a + 0.044715 * (a * a * a))))


def _peer_expert_kernel(h_ref, r2_ref, w2_ref, lr_ref, w1_ref, u_ref, vt_ref, x_ref, g_ref,
                        o_ref, acc_sc, act_sc, coef_sc):
    e = pl.program_id(1)
    heads, n_keys, _ = r2_ref.shape
    rows_per_step = u_ref.shape[0] // n_keys

    @pl.when(e == 0)
    def _():
        acc_sc[...] = jnp.zeros_like(acc_sc)

    act_sc[...] = _dot_nt(u_ref[...], h_ref[...])
    for kk in range(rows_per_step):
        i1 = e * rows_per_step + kk
        rows = slice(kk * n_keys, (kk + 1) * n_keys)
        lrows = [lr_ref[hd, pl.ds(i1, 1), :].astype(BF16) for hd in range(heads)]
        w1rows = [w1_ref[hd, pl.ds(i1, 1), :].astype(BF16) for hd in range(heads)]
        for c in range(act_sc.shape[1] // LANES):
            cols = slice(c * LANES, (c + 1) * LANES)
            cmat = None
            for hd in range(heads):
                keep = r2_ref[hd, :, cols] < lrows[hd][:, cols]
                term = jnp.where(keep, w2_ref[hd, :, cols], 0.0) * w1rows[hd][:, cols]
                cmat = term if cmat is None else cmat + term
            coef_sc[rows, cols] = cmat * _gelu_tanh(act_sc[rows, cols]).astype(BF16)
    acc_sc[...] += _dot(vt_ref[...], coef_sc[...])

    @pl.when(e == pl.num_programs(1) - 1)
    def _():
        o_ref[...] = x_ref[...] + _expand_rows(g_ref) * jnp.transpose(acc_sc[...])


def _peer_expert(h, r2, w2, lr, w1, u_bf16, vt_bf16, x, gate):
    t, d = x.shape
    heads, n_keys, _ = r2.shape
    n_exp = u_bf16.shape[0]
    tb = _pick(t, (512, 256, 128))
    eb = 1024
    row = pl.BlockSpec((tb, d), lambda i, e: (i, 0))
    sel = pl.BlockSpec((heads, n_keys, tb), lambda i, e: (0, 0, i))
    return pl.pallas_call(
        _peer_expert_kernel,
        grid=(t // tb, n_exp // eb),
        in_specs=[row, sel, sel, sel, sel,
                  pl.BlockSpec((eb, d), lambda i, e: (e, 0)),
                  pl.BlockSpec((d, eb), lambda i, e: (0, e)),
                  row, pl.BlockSpec((tb // SUBLANES, d), lambda i, e: (i, 0))],
        out_specs=row,
        out_shape=jax.ShapeDtypeStruct((t, d), F32),
        scratch_shapes=[pltpu.VMEM((d, tb), F32), pltpu.VMEM((eb, tb), F32),
                        pltpu.VMEM((eb, tb), BF16)],
        compiler_params=_cparams("arbitrary", "arbitrary"),
        name="peer_experts",
    )(h, r2, w2, lr, w1, u_bf16, vt_bf16, x, gate)


def _peer(x, gain, sc, sh, gate, w_query, sub_keys, expert_u, expert_v):
    h, s = _peer_scores(x, gain, sc, sh, jnp.transpose(w_query).astype(BF16),
                        sub_keys.astype(BF16))
    r2, w2, lr, w1 = _peer_topk(s)
    return _peer_expert(h, r2, w2, lr, w1, expert_u.astype(BF16),
                        jnp.transpose(expert_v).astype(BF16), x, gate)


def kernel(x_prompt, x_sample, cache_k_sb, cache_v_sb, cache_k_diff, cache_v_diff, page_table, c_prompt, c_sample, w_ada, b_ada, norm_mix, norm_ffn, w_qkv_sb, w_o_sb, w_qkv_diff, w_o_diff, diff_q_gain, diff_k_gain, diff_lambda_q1, diff_lambda_k1, diff_lambda_q2, diff_lambda_k2, diff_sub_gain, peer_w_query, peer_sub_keys, peer_u, peer_v):
    batch, seq, d = x_prompt.shape
    n_seq, n_new, _ = x_sample.shape
    assert d == SB_HEADS * HEAD_DIM == DIFF_HEADS * 2 * HEAD_DIM and n_new == SUBLANES
    depth = w_ada.shape[0]
    tp, ts = batch * seq, n_seq * n_new
    past_len = page_table.shape[1] * cache_k_sb.shape[2]

    x = jnp.concatenate([x_prompt.reshape(tp, d), x_sample.reshape(ts, d)], axis=0)
    c_all = jnp.concatenate([c_prompt, c_sample], axis=0)
    n_c = c_all.shape[0]
    c_all = jnp.pad(c_all, ((0, (-n_c) % SUBLANES), (0, 0)))
    mod = _ada(c_all, w_ada, b_ada)

    def per_group(layer, j):
        m = mod[layer, :, j * d:(j + 1) * d]
        return jnp.concatenate([jnp.repeat(m[:batch], seq // n_new, axis=0), m[batch:n_c]], axis=0)

    pos = jnp.concatenate([jnp.tile(jnp.arange(seq), batch),
                           jnp.tile(past_len + jnp.arange(n_new), n_seq)])
    cos_t, s1_t, s2_t = _rope_tables(pos)

    sb_k, sb_v, df_k, df_v = [], [], [], []
    for i in range(depth):
        sh_m, sc_m, g_m, sh_f, sc_f, g_f = [per_group(i, j) for j in range(6)]
        j = i // 2
        if i % 2 == 0:
            q, k, v = _qkv(x, norm_mix[i], sc_m, sh_m, w_qkv_sb[j].astype(BF16))
            o_p = _sb_prompt(q, k, v, batch, seq)
            o_s = _decode_attention("sb", q, k, v, cache_k_sb, cache_v_sb, j, page_table, tp,
                                    scale=HEAD_DIM ** -0.5)
            w_o = w_o_sb[j]
            sb_k.append(k)
            sb_v.append(v)
        else:
            lambda_init = 0.8 - 0.6 * math.exp(-0.3 * i)
            q, k, v = _qkv(x, norm_mix[i], sc_m, sh_m, w_qkv_diff[j].astype(BF16))
            q, k = _qknorm_rope(q, k, diff_q_gain[j], diff_k_gain[j], cos_t, s1_t, s2_t)
            lam_vecs = (diff_lambda_q1[j], diff_lambda_k1[j], diff_lambda_q2[j], diff_lambda_k2[j])
            o_p = _diff_prompt(q, k, v, lam_vecs, diff_sub_gain[j], lambda_init, batch, seq)
            o_s = _decode_attention(
                "diff", q, k, v, cache_k_diff, cache_v_diff, j, page_table, tp,
                extra=tuple(a.reshape(1, HEAD_DIM) for a in lam_vecs)
                + (diff_sub_gain[j].reshape(1, LANES),),
                scale=HEAD_DIM ** -0.5, lambda_init=lambda_init)
            w_o = w_o_diff[j]
            df_k.append(k)
            df_v.append(v)
        x = _outproj(jnp.concatenate([o_p, o_s], axis=0), w_o.astype(BF16), x, g_m)
        x = _peer(x, norm_ffn[i], sc_f, sh_f, g_f, peer_w_query[i], peer_sub_keys[i],
                  peer_u[i], peer_v[i])

    def rows(stack, lo, hi, shape):
        return jnp.stack([a[lo:hi].reshape(shape) for a in stack])

    t = tp + ts
    return (
        x[:tp].reshape(batch, seq, d),
        x[tp:].reshape(n_seq, n_new, d),
        rows(sb_k, 0, tp, (batch, seq, SB_HEADS, HEAD_DIM)),
        rows(sb_v, 0, tp, (batch, seq, SB_HEADS, HEAD_DIM)),
        rows(df_k, 0, tp, (batch, seq, DIFF_HEADS, 2, HEAD_DIM)),
        rows(df_v, 0, tp, (batch, seq, DIFF_HEADS, 2 * HEAD_DIM)),
        rows(sb_k, tp, t, (n_seq, n_new, SB_HEADS, HEAD_DIM)),
        rows(sb_v, tp, t, (n_seq, n_new, SB_HEADS, HEAD_DIM)),
        rows(df_k, tp, t, (n_seq, n_new, DIFF_HEADS, 2, HEAD_DIM)),
        rows(df_v, tp, t, (n_seq, n_new, DIFF_HEADS, 2 * HEAD_DIM)),
    )
```

```python
import functools
import math

import jax
import jax.numpy as jnp
from jax import lax
from jax.experimental import pallas as pl
from jax.experimental.pallas import tpu as pltpu

F32 = jnp.float32
BF16 = jnp.bfloat16

RMS_EPS = 1e-6
ROPE_THETA = 500000.0
SB_HEADS = 16
DIFF_HEADS = 8
HEAD_DIM = 64
ROPE_DIM = HEAD_DIM // 4
N_KEYS = 128
PEER_HEADS = 8
PEER_TOPK = 16
LANES = 128
SUBLANES = 8
VMEM_LIMIT = 56 * 1024 * 1024
EXP_ZERO_BELOW = -105.0


def _cparams(*sem):
    return pltpu.CompilerParams(dimension_semantics=sem, vmem_limit_bytes=VMEM_LIMIT)


def _pick(n, choices):
    for c in choices:
        if n % c == 0:
            return c
    raise ValueError(f"no block size in {choices} divides {n}")


def _dot(a, b):
    return jnp.dot(a, b, preferred_element_type=F32)


def _dot_nt(a, b):
    return lax.dot_general(a, b, (((1,), (1,)), ((), ())), preferred_element_type=F32)


def _split3(a):
    hi = a.astype(BF16)
    r = a - hi.astype(F32)
    mid = r.astype(BF16)
    lo = (r - mid.astype(F32)).astype(BF16)
    return hi, mid, lo


def _dot_f32_lhs(a, m):
    hi, mid, lo = _split3(a)
    return _dot(hi, m) + _dot(mid, m) + _dot(lo, m)


def _softplus(z):
    return jnp.maximum(z, 0.0) + jnp.log(1.0 + jnp.exp(-jnp.abs(z)))


def _expand_rows(g_ref, n=SUBLANES):
    r, d = g_ref.shape
    return jnp.concatenate([jnp.broadcast_to(g_ref[i:i + 1, :], (n, d)) for i in range(r)], axis=0)


def _norm_mod(x, gain, sc, sh):
    ms = jnp.mean(x * x, axis=-1, keepdims=True)
    y = x * lax.rsqrt(ms + RMS_EPS)
    return (y * gain) * (1.0 + sc) + sh


def _ada_kernel(c_ref, w_ref, b_ref, o_ref):
    c = c_ref[...]
    s = c / (1.0 + jnp.exp(-c))
    o_ref[0] = _dot(s.astype(BF16), w_ref[0].astype(BF16)) + b_ref[0]


def _ada(c_all, w_ada, b_ada):
    n_layers, d, n = w_ada.shape
    r = c_all.shape[0]
    tn = d
    return pl.pallas_call(
        _ada_kernel,
        grid=(n_layers, n // tn),
        in_specs=[
            pl.BlockSpec((r, d), lambda l, j: (0, 0)),
            pl.BlockSpec((1, d, tn), lambda l, j: (l, 0, j)),
            pl.BlockSpec((1, 1, tn), lambda l, j: (l, 0, j)),
        ],
        out_specs=pl.BlockSpec((1, r, tn), lambda l, j: (l, 0, j)),
        out_shape=jax.ShapeDtypeStruct((n_layers, r, n), F32),
        compiler_params=_cparams("arbitrary", "arbitrary"),
        name="ada_modulation",
    )(c_all, w_ada, b_ada.reshape(n_layers, 1, n))


def _qkv_kernel(x_ref, g_ref, sc_ref, sh_ref, w_ref, q_ref, k_ref, v_ref):
    d = x_ref.shape[1]
    h = _norm_mod(x_ref[...], g_ref[...], _expand_rows(sc_ref), _expand_rows(sh_ref)).astype(BF16)
    r = _dot(h, w_ref[...])
    q_ref[...] = r[:, :d]
    k_ref[...] = r[:, d:2 * d]
    v_ref[...] = r[:, 2 * d:]


def _qkv(x, gain, sc, sh, w_bf16):
    t, d = x.shape
    tm = _pick(t, (256, 128))
    row = pl.BlockSpec((tm, d), lambda i: (i, 0))
    grp = pl.BlockSpec((tm // SUBLANES, d), lambda i: (i, 0))
    return pl.pallas_call(
        _qkv_kernel,
        grid=(t // tm,),
        in_specs=[row, pl.BlockSpec((1, d), lambda i: (0, 0)), grp, grp,
                  pl.BlockSpec((d, 3 * d), lambda i: (0, 0))],
        out_specs=[row, row, row],
        out_shape=[jax.ShapeDtypeStruct((t, d), F32)] * 3,
        compiler_params=_cparams("arbitrary"),
        name="norm_qkv",
    )(x, gain.reshape(1, d), sc, sh, w_bf16)


def _qknorm_rope_kernel(q_ref, k_ref, qg_ref, kg_ref, c_ref, s1_ref, s2_ref, qo_ref, ko_ref):
    d = q_ref.shape[1]
    r = lax.broadcasted_iota(jnp.int32, (LANES, LANES), 0) // HEAD_DIM
    c = lax.broadcasted_iota(jnp.int32, (LANES, LANES), 1) // HEAD_DIM
    seg = jnp.where(r == c, 1.0, 0.0).astype(BF16)
    cosv, s1, s2 = c_ref[...], s1_ref[...], s2_ref[...]
    for src, gain, dst in ((q_ref, qg_ref, qo_ref), (k_ref, kg_ref, ko_ref)):
        g = gain[...]
        for j in range(d // LANES):
            x = src[:, j * LANES:(j + 1) * LANES]
            ms = _dot_f32_lhs(x * x, seg) * (1.0 / HEAD_DIM)
            y = x * lax.rsqrt(ms + RMS_EPS) * g
            up = pltpu.roll(y, LANES - ROPE_DIM // 2, 1)
            dn = pltpu.roll(y, ROPE_DIM // 2, 1)
            dst[:, j * LANES:(j + 1) * LANES] = y * cosv + up * s1 + dn * s2


def _qknorm_rope(q, k, q_gain, k_gain, cos_t, s1_t, s2_t):
    t, d = q.shape
    tm = _pick(t, (512, 256, 128))
    row = pl.BlockSpec((tm, d), lambda i: (i, 0))
    tab = pl.BlockSpec((tm, LANES), lambda i: (i, 0))
    gs = pl.BlockSpec((1, LANES), lambda i: (0, 0))
    tile = LANES // HEAD_DIM
    return pl.pallas_call(
        _qknorm_rope_kernel,
        grid=(t // tm,),
        in_specs=[row, row, gs, gs, tab, tab, tab],
        out_specs=[row, row],
        out_shape=[jax.ShapeDtypeStruct((t, d), F32)] * 2,
        compiler_params=_cparams("arbitrary"),
        name="qknorm_rope",
    )(q, k, jnp.tile(q_gain, tile).reshape(1, LANES), jnp.tile(k_gain, tile).reshape(1, LANES),
      cos_t, s1_t, s2_t)


def _rope_tables(pos):
    half = ROPE_DIM // 2
    inv = ROPE_THETA ** (-jnp.arange(half, dtype=F32) / half)
    ang = pos.astype(F32)[:, None] * inv[None, :]
    cos, sin = jnp.cos(ang), jnp.sin(ang)
    t = pos.shape[0]
    pad = jnp.zeros((t, HEAD_DIM - ROPE_DIM), F32)
    z = jnp.zeros((t, half), F32)
    c = jnp.concatenate([cos, cos, pad + 1.0], axis=1)
    s1 = jnp.concatenate([-sin, z, pad], axis=1)
    s2 = jnp.concatenate([z, sin, pad], axis=1)
    rep = LANES // HEAD_DIM
    return jnp.tile(c, (1, rep)), jnp.tile(s1, (1, rep)), jnp.tile(s2, (1, rep))


def _sb_prompt_kernel(q_ref, k_ref, v_ref, o_ref, *, tq, tk, scale):
    i = pl.program_id(2)
    q0 = i * tq
    jr = lax.broadcasted_iota(jnp.int32, (tk, tk), 0)
    jc = lax.broadcasted_iota(jnp.int32, (tk, tk), 1)
    later = jnp.where(jr > jc, 1.0, 0.0).astype(BF16)
    row = q0 + lax.broadcasted_iota(jnp.int32, (tq, 1), 0)
    n_kb = (q0 + tq) // tk
    n_heads = LANES // HEAD_DIM
    qs = [(q_ref[:, hh * HEAD_DIM:(hh + 1) * HEAD_DIM] * scale).astype(BF16)
          for hh in range(n_heads)]

    def cond(carry):
        kb, state = carry
        live = functools.reduce(jnp.maximum, [jnp.max(cs) for _, cs in state])
        return jnp.logical_and(kb >= 0, live > EXP_ZERO_BELOW)

    def body(carry):
        kb, state = carry
        k0 = pl.multiple_of(kb * tk, tk)
        kblk = k_ref[pl.ds(k0, tk), :].astype(BF16)
        vblk = v_ref[pl.ds(k0, tk), :].astype(BF16)
        col = k0 + lax.broadcasted_iota(jnp.int32, (1, tk), 1)
        causal = col < row
        new = []
        for hh, (acc, cs) in enumerate(state):
            lo, hi = hh * HEAD_DIM, (hh + 1) * HEAD_DIM
            z = _dot_nt(qs[hh], kblk[:, lo:hi])
            sp = _softplus(z)
            log_keep = jnp.where(causal, -sp, 0.0)
            after = _dot_f32_lhs(log_keep, later) + cs
            w = jnp.where(causal, jnp.exp((z - sp) + after), 0.0)
            acc = acc + _dot(w.astype(BF16), vblk[:, lo:hi])
            cs = cs + jnp.sum(log_keep, axis=1, keepdims=True)
            new.append((acc, cs))
        return kb - 1, tuple(new)

    init = tuple((jnp.zeros((tq, HEAD_DIM), F32), jnp.zeros((tq, 1), F32))
                 for _ in range(n_heads))
    _, state = lax.while_loop(cond, body, (n_kb - 1, init))
    o_ref[...] = jnp.concatenate([acc for acc, _ in state], axis=1)


def _sb_prompt(q, k, v, batch, seq):
    d = q.shape[1]
    tq = _pick(seq, (256, 128))
    tk = tq
    nq = seq // tq
    return pl.pallas_call(
        functools.partial(_sb_prompt_kernel, tq=tq, tk=tk, scale=HEAD_DIM ** -0.5),
        grid=(batch, d // LANES, nq),
        in_specs=[
            pl.BlockSpec((tq, LANES), lambda b, h, i: (b * nq + i, h)),
            pl.BlockSpec((seq, LANES), lambda b, h, i: (b, h)),
            pl.BlockSpec((seq, LANES), lambda b, h, i: (b, h)),
        ],
        out_specs=pl.BlockSpec((tq, LANES), lambda b, h, i: (b * nq + i, h)),
        out_shape=jax.ShapeDtypeStruct((batch * seq, d), F32),
        compiler_params=_cparams("arbitrary", "arbitrary", "arbitrary"),
        name="sb_prompt_attention",
    )(q, k, v)


def _lambda(lq1_ref, lk1_ref, lq2_ref, lk2_ref, lambda_init):
    a = jnp.sum(lq1_ref[...] * lk1_ref[...], axis=1, keepdims=True)
    b = jnp.sum(lq2_ref[...] * lk2_ref[...], axis=1, keepdims=True)
    return jnp.exp(a) - jnp.exp(b) + lambda_init


def _diff_prompt_kernel(q_ref, k_ref, v_ref, lq1_ref, lk1_ref, lq2_ref, lk2_ref, sg_ref, o_ref,
                        *, tq, scale, lambda_init):
    i = pl.program_id(2)
    q0 = i * tq
    tk = tq
    lam = _lambda(lq1_ref, lk1_ref, lq2_ref, lk2_ref, lambda_init)
    qs = [(q_ref[:, c * HEAD_DIM:(c + 1) * HEAD_DIM] * scale).astype(BF16) for c in range(2)]
    row = lax.broadcasted_iota(jnp.int32, (tq, 1), 0)
    col = lax.broadcasted_iota(jnp.int32, (1, tk), 1)
    diag_mask = col <= row

    def step(kb, carry, masked):
        k0 = pl.multiple_of(kb * tk, tk)
        vblk = v_ref[pl.ds(k0, tk), :].astype(BF16)
        vext = jnp.concatenate([vblk, jnp.ones_like(vblk)], axis=1)
        new = []
        for c in range(2):
            m, acc = carry[c]
            kblk = k_ref[pl.ds(k0, tk), c * HEAD_DIM:(c + 1) * HEAD_DIM].astype(BF16)
            s = _dot_nt(qs[c], kblk)
            if masked:
                s = jnp.where(diag_mask, s, -jnp.inf)
            m_new = jnp.maximum(m, jnp.max(s, axis=1, keepdims=True))
            p = jnp.exp(s - m_new)
            acc = acc * jnp.exp(m - m_new) + _dot(p.astype(BF16), vext)
            new.append((m_new, acc))
        return tuple(new)

    init = tuple((jnp.full((tq, 1), -jnp.inf, F32), jnp.zeros((tq, 2 * LANES), F32))
                 for _ in range(2))
    carry = lax.fori_loop(0, i, lambda kb, cr: step(kb, cr, False), init)
    (_, a0), (_, a1) = step(i, carry, True)
    o = a0[:, :LANES] / a0[:, LANES:] - lam * (a1[:, :LANES] / a1[:, LANES:])
    ms = jnp.mean(o * o, axis=-1, keepdims=True)
    o_ref[...] = (o * lax.rsqrt(ms + RMS_EPS) * sg_ref[...]) * (1.0 - lambda_init)


def _diff_prompt(q, k, v, lam_vecs, sub_gain, lambda_init, batch, seq):
    d = q.shape[1]
    tq = _pick(seq, (512, 256, 128))
    nq = seq // tq
    vec = pl.BlockSpec((1, HEAD_DIM), lambda b, h, i: (0, 0))
    return pl.pallas_call(
        functools.partial(_diff_prompt_kernel, tq=tq, scale=HEAD_DIM ** -0.5,
                          lambda_init=lambda_init),
        grid=(batch, d // LANES, nq),
        in_specs=[
            pl.BlockSpec((tq, LANES), lambda b, h, i: (b * nq + i, h)),
            pl.BlockSpec((seq, LANES), lambda b, h, i: (b, h)),
            pl.BlockSpec((seq, LANES), lambda b, h, i: (b, h)),
            vec, vec, vec, vec,
            pl.BlockSpec((1, LANES), lambda b, h, i: (0, 0)),
        ],
        out_specs=pl.BlockSpec((tq, LANES), lambda b, h, i: (b * nq + i, h)),
        out_shape=jax.ShapeDtypeStruct((batch * seq, d), F32),
        compiler_params=_cparams("arbitrary", "arbitrary", "arbitrary"),
        name="diff_prompt_attention",
    )(q, k, v, *[x.reshape(1, HEAD_DIM) for x in lam_vecs], sub_gain.reshape(1, LANES))


def _block_diag_queries(q, scale):
    n_q, d = q.shape
    groups = d // HEAD_DIM
    rep = jnp.concatenate([q * scale] * groups, axis=0)
    rg = lax.broadcasted_iota(jnp.int32, rep.shape, 0) // n_q
    cg = lax.broadcasted_iota(jnp.int32, rep.shape, 1) // HEAD_DIM
    return jnp.where(rg == cg, rep, 0.0).astype(BF16)


def _pad_rows(x, rows):
    return jnp.concatenate([x, jnp.zeros((rows - x.shape[0], x.shape[1]), x.dtype)], axis=0)


def _take_group_blocks(acc, n_q, width):
    rows, d = acc.shape
    a3 = acc.reshape(rows // n_q, n_q, d)
    g = lax.broadcasted_iota(jnp.int32, a3.shape, 0)
    cg = lax.broadcasted_iota(jnp.int32, a3.shape, 2) // width
    return jnp.sum(jnp.where(g == cg, a3, 0.0), axis=0)


def _load_rows_page(ref, page):
    groups = ref.shape[0] // page
    pieces = [ref[pl.ds(g, page, stride=groups), :] for g in range(groups)]
    return jnp.concatenate(pieces, axis=1)


def _new_rows_block(qbd, kn_ref, vn_ref, page, strict):
    n_q = kn_ref.shape[0]
    k_pad = _pad_rows(kn_ref[...], page).astype(BF16)
    v_pad = _pad_rows(vn_ref[...], page).astype(BF16)
    qi = lax.broadcasted_iota(jnp.int32, (LANES, page), 0) % n_q
    key = lax.broadcasted_iota(jnp.int32, (LANES, page), 1)
    valid = key < qi if strict else key <= qi
    return _dot_nt(qbd, k_pad), lambda w: _dot(w.astype(BF16), v_pad), valid


def _page_scores(qbd, kt_ref):
    return _dot(qbd, kt_ref[...].astype(BF16))


def _sb_blocks(blocks, cs):
    n = blocks[0][0].shape[1]
    jr = lax.broadcasted_iota(jnp.int32, (n, n), 0)
    jc = lax.broadcasted_iota(jnp.int32, (n, n), 1)
    later = jnp.where(jr > jc, 1.0, 0.0).astype(BF16)
    out = None
    for z, values, valid in blocks:
        sp = _softplus(z)
        log_keep = -sp if valid is None else jnp.where(valid, -sp, 0.0)
        after = _dot_f32_lhs(log_keep, later) + cs
        w = jnp.exp((z - sp) + after)
        if valid is not None:
            w = jnp.where(valid, w, 0.0)
        c = values(w)
        out = c if out is None else out + c
        cs = cs + jnp.sum(log_keep, axis=1, keepdims=True)
    return out, cs


def _sb_decode_kernel(pt_ref, q_ref, kn_ref, vn_ref, *rest, scale, pps, page):
    kp, vp, o_ref = rest[:pps], rest[pps:2 * pps], rest[2 * pps]
    qbd_sc, acc_sc, cs_sc = rest[2 * pps + 1:]
    s = pl.program_id(1)
    n_q = q_ref.shape[0]

    @pl.when(s == 0)
    def _():
        qbd = _block_diag_queries(q_ref[...], scale)
        qbd_sc[...] = qbd
        new = _new_rows_block(qbd, kn_ref, vn_ref, page, True)
        acc_sc[...], cs_sc[...] = _sb_blocks([new], jnp.zeros(cs_sc.shape, F32))

    @pl.when(jnp.max(cs_sc[...]) > EXP_ZERO_BELOW)
    def _():
        qbd = qbd_sc[...]
        blocks = [(_page_scores(qbd, kp[i]),
                   lambda w, i=i: _dot_nt(w.astype(BF16), vp[i][...].astype(BF16)), None)
                  for i in range(pps)]
        out, cs = _sb_blocks(blocks, cs_sc[...])
        acc_sc[...] += out
        cs_sc[...] = cs

    @pl.when(s == pl.num_programs(1) - 1)
    def _():
        o_ref[...] = _take_group_blocks(acc_sc[...], n_q, HEAD_DIM)


def _diff_blocks(blocks, m_old, l_old, acc_old):
    zs = [z if valid is None else jnp.where(valid, z, -jnp.inf) for z, _, valid in blocks]
    m_new = functools.reduce(jnp.maximum, [jnp.max(z, axis=1, keepdims=True) for z in zs])
    if acc_old is None:
        l, acc = jnp.zeros_like(m_new), None
    else:
        m_new = jnp.maximum(m_old, m_new)
        alpha = jnp.exp(m_old - m_new)
        l, acc = l_old * alpha, acc_old * alpha
    for (_, values, _), z in zip(blocks, zs):
        p = jnp.exp(z - m_new)
        l = l + jnp.sum(p, axis=1, keepdims=True)
        c = values(p)
        acc = c if acc is None else acc + c
    return m_new, l, acc


def _diff_decode_kernel(pt_ref, q_ref, kn_ref, vn_ref, *rest, scale, lambda_init, pps, page):
    kp, vp = rest[:pps], rest[pps:2 * pps]
    lq1_ref, lk1_ref, lq2_ref, lk2_ref, sg_ref, o_ref = rest[2 * pps:2 * pps + 6]
    qbd_sc, acc_sc, m_sc, l_sc = rest[2 * pps + 6:]
    s = pl.program_id(1)
    n_q = q_ref.shape[0]

    @pl.when(s == 0)
    def _():
        qbd = _block_diag_queries(q_ref[...], scale)
        qbd_sc[...] = qbd
        new = _new_rows_block(qbd, kn_ref, vn_ref, page, False)
        m_sc[...], l_sc[...], acc_sc[...] = _diff_blocks([new], None, None, None)

    qbd = qbd_sc[...]
    blocks = [(_page_scores(qbd, kp[i]),
               lambda w, i=i: _dot(w.astype(BF16), _load_rows_page(vp[i], page).astype(BF16)), None)
              for i in range(pps)]
    m_sc[...], l_sc[...], acc_sc[...] = _diff_blocks(blocks, m_sc[...], l_sc[...], acc_sc[...])

    @pl.when(s == pl.num_programs(1) - 1)
    def _():
        lam = _lambda(lq1_ref, lk1_ref, lq2_ref, lk2_ref, lambda_init)
        d = acc_sc.shape[1]
        full = acc_sc[...] * (1.0 / l_sc[...])
        a4 = full.reshape(DIFF_HEADS, 2, n_q, d)
        hd = lax.broadcasted_iota(jnp.int32, (DIFF_HEADS, n_q, d), 0)
        cg = lax.broadcasted_iota(jnp.int32, (DIFF_HEADS, n_q, d), 2) // LANES
        o0 = jnp.sum(jnp.where(hd == cg, a4[:, 0], 0.0), axis=0)
        o1 = jnp.sum(jnp.where(hd == cg, a4[:, 1], 0.0), axis=0)
        o = o0 - lam * o1
        sg = sg_ref[...]
        for j in range(d // LANES):
            blk = o[:, j * LANES:(j + 1) * LANES]
            ms = jnp.mean(blk * blk, axis=-1, keepdims=True)
            o_ref[:, j * LANES:(j + 1) * LANES] = (
                (blk * lax.rsqrt(ms + RMS_EPS) * sg) * (1.0 - lambda_init))


def _transposed_pages(cache):
    t = jnp.transpose(cache, (0, 1) + tuple(range(3, cache.ndim)) + (2,))
    return t.reshape(t.shape[0], t.shape[1], -1, t.shape[-1])


def _decode_attention(kind, q, k, v, cache_k, cache_v, layer, page_table, row0, extra=(), **kw):
    d = q.shape[1]
    n_seq, n_pages = page_table.shape
    n_layers, n_phys, page = cache_k.shape[:3]
    n_q = SUBLANES
    blk0 = row0 // n_q
    pps = _pick(n_pages, (4, 2, 1) if kind == "sb" else (8, 4, 2, 1))
    new = pl.BlockSpec((n_q, d), lambda b, s, pt: (blk0 + b, 0))

    def page_specs(cache):
        return [pl.BlockSpec((None, None) + cache.shape[2:],
                             lambda b, s, pt, i=i: (layer, pt[b, n_pages - 1 - (s * pps + i)], 0, 0))
                for i in range(pps)]

    cache_k = _transposed_pages(cache_k)
    kw = dict(kw, pps=pps, page=page)
    col = pltpu.VMEM((LANES, 1), F32)
    if kind == "sb":
        cache_v = _transposed_pages(cache_v)
        body = functools.partial(_sb_decode_kernel, **kw)
        extra_specs = []
        scratch = [pltpu.VMEM((LANES, d), BF16), pltpu.VMEM((LANES, d), F32), col]
    else:
        cache_v = cache_v.reshape(n_layers, n_phys, -1, cache_v.shape[-1])
        body = functools.partial(_diff_decode_kernel, **kw)
        vec = pl.BlockSpec((1, HEAD_DIM), lambda b, s, pt: (0, 0))
        extra_specs = [vec, vec, vec, vec, pl.BlockSpec((1, LANES), lambda b, s, pt: (0, 0))]
        scratch = [pltpu.VMEM((LANES, d), BF16), pltpu.VMEM((LANES, d), F32), col, col]
    return pl.pallas_call(
        body,
        grid_spec=pltpu.PrefetchScalarGridSpec(
            num_scalar_prefetch=1,
            grid=(n_seq, n_pages // pps),
            in_specs=[new, new, new] + page_specs(cache_k) + page_specs(cache_v) + extra_specs,
            out_specs=pl.BlockSpec((n_q, d), lambda b, s, pt: (b, 0)),
            scratch_shapes=scratch,
        ),
        out_shape=jax.ShapeDtypeStruct((n_seq * n_q, d), F32),
        compiler_params=_cparams("arbitrary", "arbitrary"),
        name=kind + "_decode_attention",
    )(page_table, q, k, v, *([cache_k] * pps), *([cache_v] * pps), *extra)


def _outproj_kernel(o_ref, w_ref, x_ref, g_ref, y_ref):
    y_ref[...] = x_ref[...] + _expand_rows(g_ref) * _dot(o_ref[...].astype(BF16), w_ref[...])


def _outproj(o, w_bf16, x, gate):
    t, d = x.shape
    tm = _pick(t, (512, 256, 128))
    row = pl.BlockSpec((tm, d), lambda i: (i, 0))
    return pl.pallas_call(
        _outproj_kernel,
        grid=(t // tm,),
        in_specs=[row, pl.BlockSpec((d, d), lambda i: (0, 0)), row,
                  pl.BlockSpec((tm // SUBLANES, d), lambda i: (i, 0))],
        out_specs=row,
        out_shape=jax.ShapeDtypeStruct((t, d), F32),
        compiler_params=_cparams("arbitrary"),
        name="outproj_residual",
    )(o, w_bf16, x, gate)


def _peer_scores_kernel(x_ref, g_ref, sc_ref, sh_ref, wqt_ref, keys_ref, h_ref, s_ref):
    h = _norm_mod(x_ref[...], g_ref[...], _expand_rows(sc_ref), _expand_rows(sh_ref)).astype(BF16)
    h_ref[...] = h
    qt = _dot_nt(wqt_ref[...], h)
    n_hc = s_ref.shape[0]
    half = keys_ref.shape[2]
    for hc in range(n_hc):
        s_ref[hc] = _dot(keys_ref[hc % 2], qt[hc * half:(hc + 1) * half].astype(BF16))


def _peer_scores(x, gain, sc, sh, wqt_bf16, keys_bf16):
    t, d = x.shape
    nq = wqt_bf16.shape[0]
    half = keys_bf16.shape[2]
    n_hc = nq // half
    tm = _pick(t, (256, 128))
    row = pl.BlockSpec((tm, d), lambda i: (i, 0))
    return pl.pallas_call(
        _peer_scores_kernel,
        grid=(t // tm,),
        in_specs=[row, pl.BlockSpec((1, d), lambda i: (0, 0)),
                  pl.BlockSpec((tm // SUBLANES, d), lambda i: (i, 0)),
                  pl.BlockSpec((tm // SUBLANES, d), lambda i: (i, 0)),
                  pl.BlockSpec((nq, d), lambda i: (0, 0)),
                  pl.BlockSpec(keys_bf16.shape, lambda i: (0, 0, 0))],
        out_specs=[row, pl.BlockSpec((n_hc, N_KEYS, tm), lambda i: (0, 0, i))],
        out_shape=[jax.ShapeDtypeStruct((t, d), BF16),
                   jax.ShapeDtypeStruct((n_hc, N_KEYS, t), F32)],
        compiler_params=_cparams("arbitrary"),
        name="peer_scores",
    )(x, gain.reshape(1, d), sc, sh, wqt_bf16, keys_bf16)


def _extract_topk(vals, k, break_ties):
    n = vals.shape[0]
    idx = lax.broadcasted_iota(jnp.int32, vals.shape, 0) if break_ties else None
    rank = jnp.full(vals.shape, float(k), F32)
    tops = []
    for r in range(k):
        m = jnp.max(vals, axis=0, keepdims=True)
        hit = vals == m
        if break_ties:
            first = jnp.min(jnp.where(hit, idx, n), axis=0, keepdims=True)
            hit = idx == first
        vals = jnp.where(hit, -jnp.inf, vals)
        rank = jnp.where(hit, float(r), rank)
        tops.append(m)
    return tops, rank


def _peer_select(s1, s2, k, break_ties):
    half = k // 2
    assert half % SUBLANES == 0
    t1, rank1 = _extract_topk(s1, k, break_ties)
    t2, rank2 = _extract_topk(s2, k, break_ties)
    top1 = jnp.concatenate(t1, axis=0)
    top2 = jnp.concatenate(t2, axis=0)
    cand = jnp.concatenate(
        [t1[0] + top2] + [t1[r] + top2[:half] for r in range(1, half)] + [top1[half:] + t2[0]],
        axis=0)
    tops, rank_c = _extract_topk(cand, k, break_ties)
    chosen = jnp.where(rank_c < float(k), 1.0, 0.0)
    counts = [jnp.sum(chosen[:k], axis=0, keepdims=True)]
    for r in range(1, half):
        lo = k + (r - 1) * half
        counts.append(jnp.sum(chosen[lo:lo + half], axis=0, keepdims=True))
    lo = k + (half - 1) * half
    counts += [chosen[lo + r:lo + r + 1] for r in range(half)]
    z = jnp.zeros_like(tops[0])
    for j in range(k):
        z = z + jnp.exp(tops[j] - tops[0])
    lrow = jnp.zeros(s1.shape, F32)
    for r in range(k):
        lrow = jnp.where(rank1 == float(r), counts[r], lrow)
    in1 = rank1 < float(k)
    in2 = rank2 < float(k)
    w1 = jnp.where(in1, jnp.exp(s1 - t1[0]), 0.0)
    w2 = jnp.where(in2, jnp.exp(s2 - t2[0]), 0.0) / z
    n_sel = (jnp.sum(jnp.where(in1, 1.0, 0.0), axis=0, keepdims=True)
             + jnp.sum(jnp.where(in2, 1.0, 0.0), axis=0, keepdims=True)
             + jnp.sum(chosen, axis=0, keepdims=True))
    return rank2, w2, lrow, w1, n_sel


def _peer_topk_kernel(s_ref, r2_ref, w2_ref, lr_ref, w1_ref):
    k = PEER_TOPK
    heads = r2_ref.shape[0]

    def run(break_ties):
        miss = None
        for hd in range(heads):
            s1, s2 = s_ref[2 * hd], s_ref[2 * hd + 1]
            rank2, w2, lrow, w1, n_sel = _peer_select(s1, s2, k, break_ties)
            r2_ref[hd] = rank2.astype(r2_ref.dtype)
            w2_ref[hd] = w2.astype(w2_ref.dtype)
            lr_ref[hd] = lrow
            w1_ref[hd] = w1
            d = jnp.abs(n_sel - 3.0 * k)
            miss = d if miss is None else miss + d
        return miss

    miss = run(False)
    @pl.when(jnp.max(miss) > 0.0)
    def _():
        run(True)


def _peer_topk(s):
    n_hc, n_keys, t = s.shape
    heads = n_hc // 2
    tb = LANES
    hps = _pick(heads, (4, 2, 1))
    out = pl.BlockSpec((hps, n_keys, tb), lambda i, h: (h, 0, i))
    return pl.pallas_call(
        _peer_topk_kernel,
        grid=(t // tb, heads // hps),
        in_specs=[pl.BlockSpec((2 * hps, n_keys, tb), lambda i, h: (h, 0, i))],
        out_specs=[out] * 4,
        out_shape=[jax.ShapeDtypeStruct((heads, n_keys, t), dt) for dt in (BF16, BF16, F32, F32)],
        compiler_params=_cparams("arbitrary", "arbitrary"),
        name="peer_topk",
    )(s)


def _gelu_tanh(a):
    return 0.5 * a * (1.0 + jnp.tanh(math.sqrt(2.0 / math.pi) * (a + 0.044715 * (a * a * a))))


def _peer_expert_kernel(h_ref, r2_ref, w2_ref, lr_ref, w1_ref, u_ref, vt_ref, x_ref, g_ref,
                        o_ref, acc_sc, act_sc, coef_sc):
    e = pl.program_id(1)
    heads, n_keys, _ = r2_ref.shape
    rows_per_step = u_ref.shape[0] // n_keys

    @pl.when(e == 0)
    def _():
        acc_sc[...] = jnp.zeros_like(acc_sc)

    act_sc[...] = _dot_nt(u_ref[...], h_ref[...])
    for kk in range(rows_per_step):
        i1 = e * rows_per_step + kk
        rows = slice(kk * n_keys, (kk + 1) * n_keys)
        lrows = [lr_ref[hd, pl.ds(i1, 1), :].astype(BF16) for hd in range(heads)]
        w1rows = [w1_ref[hd, pl.ds(i1, 1), :].astype(BF16) for hd in range(heads)]
        for c in range(act_sc.shape[1] // LANES):
            cols = slice(c * LANES, (c + 1) * LANES)
            cmat = None
            for hd in range(heads):
                keep = r2_ref[hd, :, cols] < lrows[hd][:, cols]
                term = jnp.where(keep, w2_ref[hd, :, cols], 0.0) * w1rows[hd][:, cols]
                cmat = term if cmat is None else cmat + term
            coef_sc[rows, cols] = cmat * _gelu_tanh(act_sc[rows, cols]).astype(BF16)
    acc_sc[...] += _dot(vt_ref[...], coef_sc[...])

    @pl.when(e == pl.num_programs(1) - 1)
    def _():
        o_ref[...] = x_ref[...] + _expand_rows(g_ref) * jnp.transpose(acc_sc[...])


def _peer_expert(h, r2, w2, lr, w1, u_bf16, vt_bf16, x, gate):
    t, d = x.shape
    heads, n_keys, _ = r2.shape
    n_exp = u_bf16.shape[0]
    tb = _pick(t, (512, 256, 128))
    eb = 1024
    row = pl.BlockSpec((tb, d), lambda i, e: (i, 0))
    sel = pl.BlockSpec((heads, n_keys, tb), lambda i, e: (0, 0, i))
    return pl.pallas_call(
        _peer_expert_kernel,
        grid=(t // tb, n_exp // eb),
        in_specs=[row, sel, sel, sel, sel,
                  pl.BlockSpec((eb, d), lambda i, e: (e, 0)),
                  pl.BlockSpec((d, eb), lambda i, e: (0, e)),
                  row, pl.BlockSpec((tb // SUBLANES, d), lambda i, e: (i, 0))],
        out_specs=row,
        out_shape=jax.ShapeDtypeStruct((t, d), F32),
        scratch_shapes=[pltpu.VMEM((d, tb), F32), pltpu.VMEM((eb, tb), F32),
                        pltpu.VMEM((eb, tb), BF16)],
        compiler_params=_cparams("arbitrary", "arbitrary"),
        name="peer_experts",
    )(h, r2, w2, lr, w1, u_bf16, vt_bf16, x, gate)


def _peer(x, gain, sc, sh, gate, w_query, sub_keys, expert_u, expert_v):
    h, s = _peer_scores(x, gain, sc, sh, jnp.transpose(w_query).astype(BF16),
                        sub_keys.astype(BF16))
    r2, w2, lr, w1 = _peer_topk(s)
    return _peer_expert(h, r2, w2, lr, w1, expert_u.astype(BF16),
                        jnp.transpose(expert_v).astype(BF16), x, gate)


def kernel(x_prompt, x_sample, cache_k_sb, cache_v_sb, cache_k_diff, cache_v_diff, page_table, c_prompt, c_sample, w_ada, b_ada, norm_mix, norm_ffn, w_qkv_sb, w_o_sb, w_qkv_diff, w_o_diff, diff_q_gain, diff_k_gain, diff_lambda_q1, diff_lambda_k1, diff_lambda_q2, diff_lambda_k2, diff_sub_gain, peer_w_query, peer_sub_keys, peer_u, peer_v):
    batch, seq, d = x_prompt.shape
    n_seq, n_new, _ = x_sample.shape
    assert d == SB_HEADS * HEAD_DIM == DIFF_HEADS * 2 * HEAD_DIM and n_new == SUBLANES
    depth = w_ada.shape[0]
    tp, ts = batch * seq, n_seq * n_new
    past_len = page_table.shape[1] * cache_k_sb.shape[2]

    x = jnp.concatenate([x_prompt.reshape(tp, d), x_sample.reshape(ts, d)], axis=0)
    c_all = jnp.concatenate([c_prompt, c_sample], axis=0)
    n_c = c_all.shape[0]
    c_all = jnp.pad(c_all, ((0, (-n_c) % SUBLANES), (0, 0)))
    mod = _ada(c_all, w_ada, b_ada)

    def per_group(layer, j):
        m = mod[layer, :, j * d:(j + 1) * d]
        return jnp.concatenate([jnp.repeat(m[:batch], seq // n_new, axis=0), m[batch:n_c]], axis=0)

    pos = jnp.concatenate([jnp.tile(jnp.arange(seq), batch),
                           jnp.tile(past_len + jnp.arange(n_new), n_seq)])
    cos_t, s1_t, s2_t = _rope_tables(pos)

    sb_k, sb_v, df_k, df_v = [], [], [], []
    for i in range(depth):
        sh_m, sc_m, g_m, sh_f, sc_f, g_f = [per_group(i, j) for j in range(6)]
        j = i // 2
        if i % 2 == 0:
            q, k, v = _qkv(x, norm_mix[i], sc_m, sh_m, w_qkv_sb[j].astype(BF16))
            o_p = _sb_prompt(q, k, v, batch, seq)
            o_s = _decode_attention("sb", q, k, v, cache_k_sb, cache_v_sb, j, page_table, tp,
                                    scale=HEAD_DIM ** -0.5)
            w_o = w_o_sb[j]
            sb_k.append(k)
            sb_v.append(v)
        else:
            lambda_init = 0.8 - 0.6 * math.exp(-0.3 * i)
            q, k, v = _qkv(x, norm_mix[i], sc_m, sh_m, w_qkv_diff[j].astype(BF16))
            q, k = _qknorm_rope(q, k, diff_q_gain[j], diff_k_gain[j], cos_t, s1_t, s2_t)
            lam_vecs = (diff_lambda_q1[j], diff_lambda_k1[j], diff_lambda_q2[j], diff_lambda_k2[j])
            o_p = _diff_prompt(q, k, v, lam_vecs, diff_sub_gain[j], lambda_init, batch, seq)
            o_s = _decode_attention(
                "diff", q, k, v, cache_k_diff, cache_v_diff, j, page_table, tp,
                extra=tuple(a.reshape(1, HEAD_DIM) for a in lam_vecs)
                + (diff_sub_gain[j].reshape(1, LANES),),
                scale=HEAD_DIM ** -0.5, lambda_init=lambda_init)
            w_o = w_o_diff[j]
            df_k.append(k)
            df_v.append(v)
        x = _outproj(jnp.concatenate([o_p, o_s], axis=0), w_o.astype(BF16), x, g_m)
        x = _peer(x, norm_ffn[i], sc_f, sh_f, g_f, peer_w_query[i], peer_sub_keys[i],
                  peer_u[i], peer_v[i])

    def rows(stack, lo, hi, shape):
        return jnp.stack([a[lo:hi].reshape(shape) for a in stack])

    t = tp + ts
    return (
        x[:tp].reshape(batch, seq, d),
        x[tp:].reshape(n_seq, n_new, d),
        rows(sb_k, 0, tp, (batch, seq, SB_HEADS, HEAD_DIM)),
        rows(sb_v, 0, tp, (batch, seq, SB_HEADS, HEAD_DIM)),
        rows(df_k, 0, tp, (batch, seq, DIFF_HEADS, 2, HEAD_DIM)),
        rows(df_v, 0, tp, (batch, seq, DIFF_HEADS, 2 * HEAD_DIM)),
        rows(sb_k, tp, t, (n_seq, n_new, SB_HEADS, HEAD_DIM)),
        rows(sb_v, tp, t, (n_seq, n_new, SB_HEADS, HEAD_DIM)),
        rows(df_k, tp, t, (n_seq, n_new, DIFF_HEADS, 2, HEAD_DIM)),
        rows(df_v, tp, t, (n_seq, n_new, DIFF_HEADS, 2 * HEAD_DIM)),
    )
```

```python
import functools
import math

import jax
import jax.numpy as jnp
from jax import lax
from jax.experimental import pallas as pl
from jax.experimental.pallas import tpu as pltpu

F32 = jnp.float32
BF16 = jnp.bfloat16

RMS_EPS = 1e-6
ROPE_THETA = 500000.0
SB_HEADS = 16
DIFF_HEADS = 8
HEAD_DIM = 64
ROPE_DIM = HEAD_DIM // 4
N_KEYS = 128
PEER_HEADS = 8
PEER_TOPK = 16
LANES = 128
SUBLANES = 8
VMEM_LIMIT = 56 * 1024 * 1024
EXP_ZERO_BELOW = -105.0


def _cparams(*sem):
    return pltpu.CompilerParams(dimension_semantics=sem, vmem_limit_bytes=VMEM_LIMIT)


def _pick(n, choices):
    for c in choices:
        if n % c == 0:
            return c
    raise ValueError(f"no block size in {choices} divides {n}")


def _dot(a, b):
    return jnp.dot(a, b, preferred_element_type=F32)


def _dot_nt(a, b):
    return lax.dot_general(a, b, (((1,), (1,)), ((), ())), preferred_element_type=F32)


def _split3(a):
    hi = a.astype(BF16)
    r = a - hi.astype(F32)
    mid = r.astype(BF16)
    lo = (r - mid.astype(F32)).astype(BF16)
    return hi, mid, lo


def _dot_f32_lhs(a, m):
    hi, mid, lo = _split3(a)
    return _dot(hi, m) + _dot(mid, m) + _dot(lo, m)


def _softplus(z):
    return jnp.maximum(z, 0.0) + jnp.log(1.0 + jnp.exp(-jnp.abs(z)))


def _expand_rows(g_ref, n=SUBLANES):
    r, d = g_ref.shape
    return jnp.concatenate([jnp.broadcast_to(g_ref[i:i + 1, :], (n, d)) for i in range(r)], axis=0)


def _norm_mod(x, gain, sc, sh):
    ms = jnp.mean(x * x, axis=-1, keepdims=True)
    y = x * lax.rsqrt(ms + RMS_EPS)
    return (y * gain) * (1.0 + sc) + sh


def _ada_kernel(c_ref, w_ref, b_ref, o_ref):
    c = c_ref[...]
    s = c / (1.0 + jnp.exp(-c))
    o_ref[0] = _dot(s.astype(BF16), w_ref[0].astype(BF16)) + b_ref[0]


def _ada(c_all, w_ada, b_ada):
    n_layers, d, n = w_ada.shape
    r = c_all.shape[0]
    tn = d
    return pl.pallas_call(
        _ada_kernel,
        grid=(n_layers, n // tn),
        in_specs=[
            pl.BlockSpec((r, d), lambda l, j: (0, 0)),
            pl.BlockSpec((1, d, tn), lambda l, j: (l, 0, j)),
            pl.BlockSpec((1, 1, tn), lambda l, j: (l, 0, j)),
        ],
        out_specs=pl.BlockSpec((1, r, tn), lambda l, j: (l, 0, j)),
        out_shape=jax.ShapeDtypeStruct((n_layers, r, n), F32),
        compiler_params=_cparams("arbitrary", "arbitrary"),
        name="ada_modulation",
    )(c_all, w_ada, b_ada.reshape(n_layers, 1, n))


def _qkv_kernel(x_ref, g_ref, sc_ref, sh_ref, w_ref, q_ref, k_ref, v_ref):
    d = x_ref.shape[1]
    h = _norm_mod(x_ref[...], g_ref[...], _expand_rows(sc_ref), _expand_rows(sh_ref)).astype(BF16)
    r = _dot(h, w_ref[...])
    q_ref[...] = r[:, :d]
    k_ref[...] = r[:, d:2 * d]
    v_ref[...] = r[:, 2 * d:]


def _qkv(x, gain, sc, sh, w_bf16):
    t, d = x.shape
    tm = _pick(t, (256, 128))
    row = pl.BlockSpec((tm, d), lambda i: (i, 0))
    grp = pl.BlockSpec((tm // SUBLANES, d), lambda i: (i, 0))
    return pl.pallas_call(
        _qkv_kernel,
        grid=(t // tm,),
        in_specs=[row, pl.BlockSpec((1, d), lambda i: (0, 0)), grp, grp,
                  pl.BlockSpec((d, 3 * d), lambda i: (0, 0))],
        out_specs=[row, row, row],
        out_shape=[jax.ShapeDtypeStruct((t, d), F32)] * 3,
        compiler_params=_cparams("arbitrary"),
        name="norm_qkv",
    )(x, gain.reshape(1, d), sc, sh, w_bf16)


def _qknorm_rope_kernel(q_ref, k_ref, qg_ref, kg_ref, c_ref, s1_ref, s2_ref, qo_ref, ko_ref):
    d = q_ref.shape[1]
    r = lax.broadcasted_iota(jnp.int32, (LANES, LANES), 0) // HEAD_DIM
    c = lax.broadcasted_iota(jnp.int32, (LANES, LANES), 1) // HEAD_DIM
    seg = jnp.where(r == c, 1.0, 0.0).astype(BF16)
    cosv, s1, s2 = c_ref[...], s1_ref[...], s2_ref[...]
    for src, gain, dst in ((q_ref, qg_ref, qo_ref), (k_ref, kg_ref, ko_ref)):
        g = gain[...]
        for j in range(d // LANES):
            x = src[:, j * LANES:(j + 1) * LANES]
            ms = _dot_f32_lhs(x * x, seg) * (1.0 / HEAD_DIM)
            y = x * lax.rsqrt(ms + RMS_EPS) * g
            up = pltpu.roll(y, LANES - ROPE_DIM // 2, 1)
            dn = pltpu.roll(y, ROPE_DIM // 2, 1)
            dst[:, j * LANES:(j + 1) * LANES] = y * cosv + up * s1 + dn * s2


def _qknorm_rope(q, k, q_gain, k_gain, cos_t, s1_t, s2_t):
    t, d = q.shape
    tm = _pick(t, (512, 256, 128))
    row = pl.BlockSpec((tm, d), lambda i: (i, 0))
    tab = pl.BlockSpec((tm, LANES), lambda i: (i, 0))
    gs = pl.BlockSpec((1, LANES), lambda i: (0, 0))
    tile = LANES // HEAD_DIM
    return pl.pallas_call(
        _qknorm_rope_kernel,
        grid=(t // tm,),
        in_specs=[row, row, gs, gs, tab, tab, tab],
        out_specs=[row, row],
        out_shape=[jax.ShapeDtypeStruct((t, d), F32)] * 2,
        compiler_params=_cparams("arbitrary"),
        name="qknorm_rope",
    )(q, k, jnp.tile(q_gain, tile).reshape(1, LANES), jnp.tile(k_gain, tile).reshape(1, LANES),
      cos_t, s1_t, s2_t)


def _rope_tables(pos):
    half = ROPE_DIM // 2
    inv = ROPE_THETA ** (-jnp.arange(half, dtype=F32) / half)
    ang = pos.astype(F32)[:, None] * inv[None, :]
    cos, sin = jnp.cos(ang), jnp.sin(ang)
    t = pos.shape[0]
    pad = jnp.zeros((t, HEAD_DIM - ROPE_DIM), F32)
    z = jnp.zeros((t, half), F32)
    c = jnp.concatenate([cos, cos, pad + 1.0], axis=1)
    s1 = jnp.concatenate([-sin, z, pad], axis=1)
    s2 = jnp.concatenate([z, sin, pad], axis=1)
    rep = LANES // HEAD_DIM
    return jnp.tile(c, (1, rep)), jnp.tile(s1, (1, rep)), jnp.tile(s2, (1, rep))


def _sb_prompt_kernel(q_ref, k_ref, v_ref, o_ref, *, tq, tk, scale):
    i = pl.program_id(2)
    q0 = i * tq
    jr = lax.broadcasted_iota(jnp.int32, (tk, tk), 0)
    jc = lax.broadcasted_iota(jnp.int32, (tk, tk), 1)
    later = jnp.where(jr > jc, 1.0, 0.0).astype(BF16)
    row = q0 + lax.broadcasted_iota(jnp.int32, (tq, 1), 0)
    n_kb = (q0 + tq) // tk
    n_heads = LANES // HEAD_DIM
    qs = [(q_ref[:, hh * HEAD_DIM:(hh + 1) * HEAD_DIM] * scale).astype(BF16)
          for hh in range(n_heads)]

    def cond(carry):
        kb, state = carry
        live = functools.reduce(jnp.maximum, [jnp.max(cs) for _, cs in state])
        return jnp.logical_and(kb >= 0, live > EXP_ZERO_BELOW)

    def body(carry):
        kb, state = carry
        k0 = pl.multiple_of(kb * tk, tk)
        kblk = k_ref[pl.ds(k0, tk), :].astype(BF16)
        vblk = v_ref[pl.ds(k0, tk), :].astype(BF16)
        col = k0 + lax.broadcasted_iota(jnp.int32, (1, tk), 1)
        causal = col < row
        new = []
        for hh, (acc, cs) in enumerate(state):
            lo, hi = hh * HEAD_DIM, (hh + 1) * HEAD_DIM
            z = _dot_nt(qs[hh], kblk[:, lo:hi])
            sp = _softplus(z)
            log_keep = jnp.where(causal, -sp, 0.0)
            after = _dot_f32_lhs(log_keep, later) + cs
            w = jnp.where(causal, jnp.exp((z - sp) + after), 0.0)
            acc = acc + _dot(w.astype(BF16), vblk[:, lo:hi])
            cs = cs + jnp.sum(log_keep, axis=1, keepdims=True)
            new.append((acc, cs))
        return kb - 1, tuple(new)

    init = tuple((jnp.zeros((tq, HEAD_DIM), F32), jnp.zeros((tq, 1), F32))
                 for _ in range(n_heads))
    _, state = lax.while_loop(cond, body, (n_kb - 1, init))
    o_ref[...] = jnp.concatenate([acc for acc, _ in state], axis=1)


def _sb_prompt(q, k, v, batch, seq):
    d = q.shape[1]
    tq = _pick(seq, (256, 128))
    tk = tq
    nq = seq // tq
    return pl.pallas_call(
        functools.partial(_sb_prompt_kernel, tq=tq, tk=tk, scale=HEAD_DIM ** -0.5),
        grid=(batch, d // LANES, nq),
        in_specs=[
            pl.BlockSpec((tq, LANES), lambda b, h, i: (b * nq + i, h)),
            pl.BlockSpec((seq, LANES), lambda b, h, i: (b, h)),
            pl.BlockSpec((seq, LANES), lambda b, h, i: (b, h)),
        ],
        out_specs=pl.BlockSpec((tq, LANES), lambda b, h, i: (b * nq + i, h)),
        out_shape=jax.ShapeDtypeStruct((batch * seq, d), F32),
        compiler_params=_cparams("arbitrary", "arbitrary", "arbitrary"),
        name="sb_prompt_attention",
    )(q, k, v)


def _lambda(lq1_ref, lk1_ref, lq2_ref, lk2_ref, lambda_init):
    a = jnp.sum(lq1_ref[...] * lk1_ref[...], axis=1, keepdims=True)
    b = jnp.sum(lq2_ref[...] * lk2_ref[...], axis=1, keepdims=True)
    return jnp.exp(a) - jnp.exp(b) + lambda_init


def _diff_prompt_kernel(q_ref, k_ref, v_ref, lq1_ref, lk1_ref, lq2_ref, lk2_ref, sg_ref, o_ref,
                        *, tq, scale, lambda_init):
    i = pl.program_id(2)
    q0 = i * tq
    tk = tq
    lam = _lambda(lq1_ref, lk1_ref, lq2_ref, lk2_ref, lambda_init)
    qs = [(q_ref[:, c * HEAD_DIM:(c + 1) * HEAD_DIM] * scale).astype(BF16) for c in range(2)]
    row = lax.broadcasted_iota(jnp.int32, (tq, 1), 0)
    col = lax.broadcasted_iota(jnp.int32, (1, tk), 1)
    diag_mask = col <= row

    def step(kb, carry, masked):
        k0 = pl.multiple_of(kb * tk, tk)
        vblk = v_ref[pl.ds(k0, tk), :].astype(BF16)
        vext = jnp.concatenate([vblk, jnp.ones_like(vblk)], axis=1)
        new = []
        for c in range(2):
            m, acc = carry[c]
            kblk = k_ref[pl.ds(k0, tk), c * HEAD_DIM:(c + 1) * HEAD_DIM].astype(BF16)
            s = _dot_nt(qs[c], kblk)
            if masked:
                s = jnp.where(diag_mask, s, -jnp.inf)
            m_new = jnp.maximum(m, jnp.max(s, axis=1, keepdims=True))
            p = jnp.exp(s - m_new)
            acc = acc * jnp.exp(m - m_new) + _dot(p.astype(BF16), vext)
            new.append((m_new, acc))
        return tuple(new)

    init = tuple((jnp.full((tq, 1), -jnp.inf, F32), jnp.zeros((tq, 2 * LANES), F32))
                 for _ in range(2))
    carry = lax.fori_loop(0, i, lambda kb, cr: step(kb, cr, False), init)
    (_, a0), (_, a1) = step(i, carry, True)
    o = a0[:, :LANES] / a0[:, LANES:] - lam * (a1[:, :LANES] / a1[:, LANES:])
    ms = jnp.mean(o * o, axis=-1, keepdims=True)
    o_ref[...] = (o * lax.rsqrt(ms + RMS_EPS) * sg_ref[...]) * (1.0 - lambda_init)


def _diff_prompt(q, k, v, lam_vecs, sub_gain, lambda_init, batch, seq):
    d = q.shape[1]
    tq = _pick(seq, (512, 256, 128))
    nq = seq // tq
    vec = pl.BlockSpec((1, HEAD_DIM), lambda b, h, i: (0, 0))
    return pl.pallas_call(
        functools.partial(_diff_prompt_kernel, tq=tq, scale=HEAD_DIM ** -0.5,
                          lambda_init=lambda_init),
        grid=(batch, d // LANES, nq),
        in_specs=[
            pl.BlockSpec((tq, LANES), lambda b, h, i: (b * nq + i, h)),
            pl.BlockSpec((seq, LANES), lambda b, h, i: (b, h)),
            pl.BlockSpec((seq, LANES), lambda b, h, i: (b, h)),
            vec, vec, vec, vec,
            pl.BlockSpec((1, LANES), lambda b, h, i: (0, 0)),
        ],
        out_specs=pl.BlockSpec((tq, LANES), lambda b, h, i: (b * nq + i, h)),
        out_shape=jax.ShapeDtypeStruct((batch * seq, d), F32),
        compiler_params=_cparams("arbitrary", "arbitrary", "arbitrary"),
        name="diff_prompt_attention",
    )(q, k, v, *[x.reshape(1, HEAD_DIM) for x in lam_vecs], sub_gain.reshape(1, LANES))


def _block_diag_queries(q, scale):
    n_q, d = q.shape
    groups = d // HEAD_DIM
    rep = jnp.concatenate([q * scale] * groups, axis=0)
    rg = lax.broadcasted_iota(jnp.int32, rep.shape, 0) // n_q
    cg = lax.broadcasted_iota(jnp.int32, rep.shape, 1) // HEAD_DIM
    return jnp.where(rg == cg, rep, 0.0).astype(BF16)


def _pad_rows(x, rows):
    return jnp.concatenate([x, jnp.zeros((rows - x.shape[0], x.shape[1]), x.dtype)], axis=0)


def _take_group_blocks(acc, n_q, width):
    rows, d = acc.shape
    a3 = acc.reshape(rows // n_q, n_q, d)
    g = lax.broadcasted_iota(jnp.int32, a3.shape, 0)
    cg = lax.broadcasted_iota(jnp.int32, a3.shape, 2) // width
    return jnp.sum(jnp.where(g == cg, a3, 0.0), axis=0)


def _load_rows_page(ref, page):
    groups = ref.shape[0] // page
    pieces = [ref[pl.ds(g, page, stride=groups), :] for g in range(groups)]
    return jnp.concatenate(pieces, axis=1)


def _new_rows_block(qbd, kn_ref, vn_ref, page, strict):
    n_q = kn_ref.shape[0]
    k_pad = _pad_rows(kn_ref[...], page).astype(BF16)
    v_pad = _pad_rows(vn_ref[...], page).astype(BF16)
    qi = lax.broadcasted_iota(jnp.int32, (LANES, page), 0) % n_q
    key = lax.broadcasted_iota(jnp.int32, (LANES, page), 1)
    valid = key < qi if strict else key <= qi
    return _dot_nt(qbd, k_pad), lambda w: _dot(w.astype(BF16), v_pad), valid


def _page_scores(qbd, kt_ref):
    return _dot(qbd, kt_ref[...].astype(BF16))


def _sb_blocks(blocks, cs):
    n = blocks[0][0].shape[1]
    jr = lax.broadcasted_iota(jnp.int32, (n, n), 0)
    jc = lax.broadcasted_iota(jnp.int32, (n, n), 1)
    later = jnp.where(jr > jc, 1.0, 0.0).astype(BF16)
    out = None
    for z, values, valid in blocks:
        sp = _softplus(z)
        log_keep = -sp if valid is None else jnp.where(valid, -sp, 0.0)
        after = _dot_f32_lhs(log_keep, later) + cs
        w = jnp.exp((z - sp) + after)
        if valid is not None:
            w = jnp.where(valid, w, 0.0)
        c = values(w)
        out = c if out is None else out + c
        cs = cs + jnp.sum(log_keep, axis=1, keepdims=True)
    return out, cs


def _sb_decode_kernel(pt_ref, fl_ref, q_ref, kn_ref, vn_ref, cs_in_ref, o_in_ref, *rest,
                      scale, pps, page, first):
    kp, vp = rest[:pps], rest[pps:2 * pps]
    o_ref, cs_out_ref = rest[2 * pps:2 * pps + 2]
    qbd_sc, acc_sc, cs_sc = rest[2 * pps + 2:]
    b = pl.program_id(0)
    s = pl.program_id(1)
    n_q = q_ref.shape[0]

    @pl.when(s == 0)
    def _():
        qbd = _block_diag_queries(q_ref[...], scale)
        qbd_sc[...] = qbd
        if first:
            new = _new_rows_block(qbd, kn_ref, vn_ref, page, True)
            acc_sc[...], cs_sc[...] = _sb_blocks([new], jnp.zeros(cs_sc.shape, F32))
        else:
            acc_sc[...] = jnp.zeros_like(acc_sc)
            cs_sc[...] = cs_in_ref[...]

    live = jnp.max(cs_sc[...]) > EXP_ZERO_BELOW
    if not first:
        live = jnp.logical_and(live, fl_ref[b] > 0)

    @pl.when(live)
    def _():
        qbd = qbd_sc[...]
        blocks = [(_page_scores(qbd, kp[i]),
                   lambda w, i=i: _dot_nt(w.astype(BF16), vp[i][...].astype(BF16)), None)
                  for i in range(pps)]
        out, cs = _sb_blocks(blocks, cs_sc[...])
        acc_sc[...] += out
        cs_sc[...] = cs

    @pl.when(s == pl.num_programs(1) - 1)
    def _():
        o = _take_group_blocks(acc_sc[...], n_q, HEAD_DIM)
        o_ref[...] = o if first else o + o_in_ref[...]
        cs_out_ref[...] = cs_sc[...]


def _sb_decode(q, k, v, cache_k, cache_v, layer, page_table, row0, scale):
    d = q.shape[1]
    n_seq, n_pages = page_table.shape
    page = cache_k.shape[2]
    n_q = SUBLANES
    blk0 = row0 // n_q
    pps = _pick(n_pages, (4, 2, 1))
    ck, cv = _transposed_pages(cache_k), _transposed_pages(cache_v)

    def call(first, flags, cs_in, o_in, n_steps, page0):
        def page_specs(cache):
            def imap(b, s, pt, fl, i):
                want = n_pages - 1 - (page0 + s * pps + i)
                idx = want if first else jnp.where(fl[b] > 0, want, n_pages - 1 - page0)
                return (layer, pt[b, idx], 0, 0)
            return [pl.BlockSpec((None, None) + cache.shape[2:], functools.partial(imap, i=i))
                    for i in range(pps)]
        new = pl.BlockSpec((n_q, d), lambda b, s, pt, fl: (blk0 + b, 0))
        col = pl.BlockSpec((None, LANES, 1), lambda b, s, pt, fl: (b, 0, 0))
        row = pl.BlockSpec((n_q, d), lambda b, s, pt, fl: (b, 0))
        return pl.pallas_call(
            functools.partial(_sb_decode_kernel, scale=scale, pps=pps, page=page, first=first),
            grid_spec=pltpu.PrefetchScalarGridSpec(
                num_scalar_prefetch=2,
                grid=(n_seq, n_steps),
                in_specs=[new, new, new, col, row] + page_specs(ck) + page_specs(cv),
                out_specs=[row, col],
                scratch_shapes=[pltpu.VMEM((LANES, d), BF16), pltpu.VMEM((LANES, d), F32),
                                pltpu.VMEM((LANES, 1), F32)],
            ),
            out_shape=[jax.ShapeDtypeStruct((n_seq * n_q, d), F32),
                       jax.ShapeDtypeStruct((n_seq, LANES, 1), F32)],
            compiler_params=_cparams("arbitrary", "arbitrary"),
            name="sb_decode_attention_" + ("new" if first else "old"),
        )(page_table, flags, q, k, v, cs_in, o_in, *([ck] * pps), *([cv] * pps))

    zero_cs = jnp.zeros((n_seq, LANES, 1), F32)
    zero_o = jnp.zeros((n_seq * n_q, d), F32)
    o1, cs1 = call(True, jnp.ones((n_seq,), jnp.int32), zero_cs, zero_o, 1, 0)
    if n_pages == pps:
        return o1
    flags = (jnp.max(cs1, axis=(1, 2)) > EXP_ZERO_BELOW).astype(jnp.int32)
    o2, _ = call(False, flags, cs1, o1, n_pages // pps - 1, pps)
    return o2


def _diff_blocks(blocks, m_old, l_old, acc_old):
    zs = [z if valid is None else jnp.where(valid, z, -jnp.inf) for z, _, valid in blocks]
    m_new = functools.reduce(jnp.maximum, [jnp.max(z, axis=1, keepdims=True) for z in zs])
    if acc_old is None:
        l, acc = jnp.zeros_like(m_new), None
    else:
        m_new = jnp.maximum(m_old, m_new)
        alpha = jnp.exp(m_old - m_new)
        l, acc = l_old * alpha, acc_old * alpha
    for (_, values, _), z in zip(blocks, zs):
        p = jnp.exp(z - m_new)
        l = l + jnp.sum(p, axis=1, keepdims=True)
        c = values(p)
        acc = c if acc is None else acc + c
    return m_new, l, acc


def _diff_decode_kernel(pt_ref, q_ref, kn_ref, vn_ref, *rest, scale, lambda_init, pps, page):
    kp, vp = rest[:pps], rest[pps:2 * pps]
    lq1_ref, lk1_ref, lq2_ref, lk2_ref, sg_ref, o_ref = rest[2 * pps:2 * pps + 6]
    qbd_sc, acc_sc, m_sc, l_sc = rest[2 * pps + 6:]
    s = pl.program_id(1)
    n_q = q_ref.shape[0]

    @pl.when(s == 0)
    def _():
        qbd = _block_diag_queries(q_ref[...], scale)
        qbd_sc[...] = qbd
        new = _new_rows_block(qbd, kn_ref, vn_ref, page, False)
        m_sc[...], l_sc[...], acc_sc[...] = _diff_blocks([new], None, None, None)

    qbd = qbd_sc[...]
    blocks = [(_page_scores(qbd, kp[i]),
               lambda w, i=i: _dot(w.astype(BF16), _load_rows_page(vp[i], page).astype(BF16)), None)
              for i in range(pps)]
    m_sc[...], l_sc[...], acc_sc[...] = _diff_blocks(blocks, m_sc[...], l_sc[...], acc_sc[...])

    @pl.when(s == pl.num_programs(1) - 1)
    def _():
        lam = _lambda(lq1_ref, lk1_ref, lq2_ref, lk2_ref, lambda_init)
        d = acc_sc.shape[1]
        full = acc_sc[...] * (1.0 / l_sc[...])
        a4 = full.reshape(DIFF_HEADS, 2, n_q, d)
        hd = lax.broadcasted_iota(jnp.int32, (DIFF_HEADS, n_q, d), 0)
        cg = lax.broadcasted_iota(jnp.int32, (DIFF_HEADS, n_q, d), 2) // LANES
        o0 = jnp.sum(jnp.where(hd == cg, a4[:, 0], 0.0), axis=0)
        o1 = jnp.sum(jnp.where(hd == cg, a4[:, 1], 0.0), axis=0)
        o = o0 - lam * o1
        sg = sg_ref[...]
        for j in range(d // LANES):
            blk = o[:, j * LANES:(j + 1) * LANES]
            ms = jnp.mean(blk * blk, axis=-1, keepdims=True)
            o_ref[:, j * LANES:(j + 1) * LANES] = (
                (blk * lax.rsqrt(ms + RMS_EPS) * sg) * (1.0 - lambda_init))


def _transposed_pages(cache):
    t = jnp.transpose(cache, (0, 1) + tuple(range(3, cache.ndim)) + (2,))
    return t.reshape(t.shape[0], t.shape[1], -1, t.shape[-1])


def _decode_attention(kind, q, k, v, cache_k, cache_v, layer, page_table, row0, extra=(), **kw):
    d = q.shape[1]
    n_seq, n_pages = page_table.shape
    n_layers, n_phys, page = cache_k.shape[:3]
    n_q = SUBLANES
    blk0 = row0 // n_q
    pps = _pick(n_pages, (4, 2, 1) if kind == "sb" else (8, 4, 2, 1))
    new = pl.BlockSpec((n_q, d), lambda b, s, pt: (blk0 + b, 0))

    def page_specs(cache):
        return [pl.BlockSpec((None, None) + cache.shape[2:],
                             lambda b, s, pt, i=i: (layer, pt[b, n_pages - 1 - (s * pps + i)], 0, 0))
                for i in range(pps)]

    cache_k = _transposed_pages(cache_k)
    kw = dict(kw, pps=pps, page=page)
    col = pltpu.VMEM((LANES, 1), F32)
    if kind == "sb":
        cache_v = _transposed_pages(cache_v)
        body = functools.partial(_sb_decode_kernel, **kw)
        extra_specs = []
        scratch = [pltpu.VMEM((LANES, d), BF16), pltpu.VMEM((LANES, d), F32), col]
    else:
        cache_v = cache_v.reshape(n_layers, n_phys, -1, cache_v.shape[-1])
        body = functools.partial(_diff_decode_kernel, **kw)
        vec = pl.BlockSpec((1, HEAD_DIM), lambda b, s, pt: (0, 0))
        extra_specs = [vec, vec, vec, vec, pl.BlockSpec((1, LANES), lambda b, s, pt: (0, 0))]
        scratch = [pltpu.VMEM((LANES, d), BF16), pltpu.VMEM((LANES, d), F32), col, col]
    return pl.pallas_call(
        body,
        grid_spec=pltpu.PrefetchScalarGridSpec(
            num_scalar_prefetch=1,
            grid=(n_seq, n_pages // pps),
            in_specs=[new, new, new] + page_specs(cache_k) + page_specs(cache_v) + extra_specs,
            out_specs=pl.BlockSpec((n_q, d), lambda b, s, pt: (b, 0)),
            scratch_shapes=scratch,
        ),
        out_shape=jax.ShapeDtypeStruct((n_seq * n_q, d), F32),
        compiler_params=_cparams("arbitrary", "arbitrary"),
        name=kind + "_decode_attention",
    )(page_table, q, k, v, *([cache_k] * pps), *([cache_v] * pps), *extra)


def _outproj_kernel(o_ref, w_ref, x_ref, g_ref, y_ref):
    y_ref[...] = x_ref[...] + _expand_rows(g_ref) * _dot(o_ref[...].astype(BF16), w_ref[...])


def _outproj(o, w_bf16, x, gate):
    t, d = x.shape
    tm = _pick(t, (512, 256, 128))
    row = pl.BlockSpec((tm, d), lambda i: (i, 0))
    return pl.pallas_call(
        _outproj_kernel,
        grid=(t // tm,),
        in_specs=[row, pl.BlockSpec((d, d), lambda i: (0, 0)), row,
                  pl.BlockSpec((tm // SUBLANES, d), lambda i: (i, 0))],
        out_specs=row,
        out_shape=jax.ShapeDtypeStruct((t, d), F32),
        compiler_params=_cparams("arbitrary"),
        name="outproj_residual",
    )(o, w_bf16, x, gate)


def _peer_scores_kernel(x_ref, g_ref, sc_ref, sh_ref, wqt_ref, keys_ref, h_ref, s_ref):
    h = _norm_mod(x_ref[...], g_ref[...], _expand_rows(sc_ref), _expand_rows(sh_ref)).astype(BF16)
    h_ref[...] = h
    qt = _dot_nt(wqt_ref[...], h)
    n_hc = s_ref.shape[0]
    half = keys_ref.shape[2]
    for hc in range(n_hc):
        s_ref[hc] = _dot(keys_ref[hc % 2], qt[hc * half:(hc + 1) * half].astype(BF16))


def _peer_scores(x, gain, sc, sh, wqt_bf16, keys_bf16):
    t, d = x.shape
    nq = wqt_bf16.shape[0]
    half = keys_bf16.shape[2]
    n_hc = nq // half
    tm = _pick(t, (256, 128))
    row = pl.BlockSpec((tm, d), lambda i: (i, 0))
    return pl.pallas_call(
        _peer_scores_kernel,
        grid=(t // tm,),
        in_specs=[row, pl.BlockSpec((1, d), lambda i: (0, 0)),
                  pl.BlockSpec((tm // SUBLANES, d), lambda i: (i, 0)),
                  pl.BlockSpec((tm // SUBLANES, d), lambda i: (i, 0)),
                  pl.BlockSpec((nq, d), lambda i: (0, 0)),
                  pl.BlockSpec(keys_bf16.shape, lambda i: (0, 0, 0))],
        out_specs=[row, pl.BlockSpec((n_hc, N_KEYS, tm), lambda i: (0, 0, i))],
        out_shape=[jax.ShapeDtypeStruct((t, d), BF16),
                   jax.ShapeDtypeStruct((n_hc, N_KEYS, t), F32)],
        compiler_params=_cparams("arbitrary"),
        name="peer_scores",
    )(x, gain.reshape(1, d), sc, sh, wqt_bf16, keys_bf16)


def _extract_topk(vals, k, break_ties):
    n = vals.shape[0]
    idx = lax.broadcasted_iota(jnp.int32, vals.shape, 0) if break_ties else None
    rank = jnp.full(vals.shape, float(k), F32)
    tops = []
    for r in range(k):
        m = jnp.max(vals, axis=0, keepdims=True)
        hit = vals == m
        if break_ties:
            first = jnp.min(jnp.where(hit, idx, n), axis=0, keepdims=True)
            hit = idx == first
        vals = jnp.where(hit, -jnp.inf, vals)
        rank = jnp.where(hit, float(r), rank)
        tops.append(m)
    return tops, rank


def _peer_select(s1, s2, k, break_ties):
    half = k // 2
    assert half % SUBLANES == 0
    t1, rank1 = _extract_topk(s1, k, break_ties)
    t2, rank2 = _extract_topk(s2, k, break_ties)
    top1 = jnp.concatenate(t1, axis=0)
    top2 = jnp.concatenate(t2, axis=0)
    cand = jnp.concatenate(
        [t1[0] + top2] + [t1[r] + top2[:half] for r in range(1, half)] + [top1[half:] + t2[0]],
        axis=0)
    tops, rank_c = _extract_topk(cand, k, break_ties)
    chosen = jnp.where(rank_c < float(k), 1.0, 0.0)
    counts = [jnp.sum(chosen[:k], axis=0, keepdims=True)]
    for r in range(1, half):
        lo = k + (r - 1) * half
        counts.append(jnp.sum(chosen[lo:lo + half], axis=0, keepdims=True))
    lo = k + (half - 1) * half
    counts += [chosen[lo + r:lo + r + 1] for r in range(half)]
    z = jnp.zeros_like(tops[0])
    for j in range(k):
        z = z + jnp.exp(tops[j] - tops[0])
    lrow = jnp.zeros(s1.shape, F32)
    for r in range(k):
        lrow = jnp.where(rank1 == float(r), counts[r], lrow)
    in1 = rank1 < float(k)
    in2 = rank2 < float(k)
    w1 = jnp.where(in1, jnp.exp(s1 - t1[0]), 0.0)
    w2 = jnp.where(in2, jnp.exp(s2 - t2[0]), 0.0) / z
    n_sel = (jnp.sum(jnp.where(in1, 1.0, 0.0), axis=0, keepdims=True)
             + jnp.sum(jnp.where(in2, 1.0, 0.0), axis=0, keepdims=True)
             + jnp.sum(chosen, axis=0, keepdims=True))
    return rank2, w2, lrow, w1, n_sel


def _peer_topk_kernel(s_ref, r2_ref, w2_ref, lr_ref, w1_ref):
    k = PEER_TOPK
    heads = r2_ref.shape[0]

    def run(break_ties):
        miss = None
        for hd in range(heads):
            s1, s2 = s_ref[2 * hd], s_ref[2 * hd + 1]
            rank2, w2, lrow, w1, n_sel = _peer_select(s1, s2, k, break_ties)
            r2_ref[hd] = rank2.astype(r2_ref.dtype)
            w2_ref[hd] = w2.astype(w2_ref.dtype)
            lr_ref[hd] = lrow
            w1_ref[hd] = w1
            d = jnp.abs(n_sel - 3.0 * k)
            miss = d if miss is None else miss + d
        return miss

    miss = run(False)
    @pl.when(jnp.max(miss) > 0.0)
    def _():
        run(True)


def _peer_topk(s):
    n_hc, n_keys, t = s.shape
    heads = n_hc // 2
    tb = LANES
    hps = _pick(heads, (4, 2, 1))
    out = pl.BlockSpec((hps, n_keys, tb), lambda i, h: (h, 0, i))
    return pl.pallas_call(
        _peer_topk_kernel,
        grid=(t // tb, heads // hps),
        in_specs=[pl.BlockSpec((2 * hps, n_keys, tb), lambda i, h: (h, 0, i))],
        out_specs=[out] * 4,
        out_shape=[jax.ShapeDtypeStruct((heads, n_keys, t), dt) for dt in (BF16, BF16, F32, F32)],
        compiler_params=_cparams("arbitrary", "arbitrary"),
        name="peer_topk",
    )(s)


def _gelu_tanh(a):
    return 0.5 * a * (1.0 + jnp.tanh(math.sqrt(2.0 / math.pi) * (a + 0.044715 * (a * a * a))))


def _peer_expert_kernel(h_ref, r2_ref, w2_ref, lr_ref, w1_ref, u_ref, vt_ref, x_ref, g_ref,
                        o_ref, acc_sc, act_sc, coef_sc):
    e = pl.program_id(1)
    heads, n_keys, _ = r2_ref.shape
    rows_per_step = u_ref.shape[0] // n_keys

    @pl.when(e == 0)
    def _():
        acc_sc[...] = jnp.zeros_like(acc_sc)

    act_sc[...] = _dot_nt(u_ref[...], h_ref[...])
    for kk in range(rows_per_step):
        i1 = e * rows_per_step + kk
        rows = slice(kk * n_keys, (kk + 1) * n_keys)
        lrows = [lr_ref[hd, pl.ds(i1, 1), :].astype(BF16) for hd in range(heads)]
        w1rows = [w1_ref[hd, pl.ds(i1, 1), :].astype(BF16) for hd in range(heads)]
        for c in range(act_sc.shape[1] // LANES):
            cols = slice(c * LANES, (c + 1) * LANES)
            cmat = None
            for hd in range(heads):
                keep = r2_ref[hd, :, cols] < lrows[hd][:, cols]
                term = jnp.where(keep, w2_ref[hd, :, cols], 0.0) * w1rows[hd][:, cols]
                cmat = term if cmat is None else cmat + term
            coef_sc[rows, cols] = cmat * _gelu_tanh(act_sc[rows, cols]).astype(BF16)
    acc_sc[...] += _dot(vt_ref[...], coef_sc[...])

    @pl.when(e == pl.num_programs(1) - 1)
    def _():
        o_ref[...] = x_ref[...] + _expand_rows(g_ref) * jnp.transpose(acc_sc[...])


def _peer_expert(h, r2, w2, lr, w1, u_bf16, vt_bf16, x, gate):
    t, d = x.shape
    heads, n_keys, _ = r2.shape
    n_exp = u_bf16.shape[0]
    tb = _pick(t, (512, 256, 128))
    eb = 1024
    row = pl.BlockSpec((tb, d), lambda i, e: (i, 0))
    sel = pl.BlockSpec((heads, n_keys, tb), lambda i, e: (0, 0, i))
    return pl.pallas_call(
        _peer_expert_kernel,
        grid=(t // tb, n_exp // eb),
        in_specs=[row, sel, sel, sel, sel,
                  pl.BlockSpec((eb, d), lambda i, e: (e, 0)),
                  pl.BlockSpec((d, eb), lambda i, e: (0, e)),
                  row, pl.BlockSpec((tb // SUBLANES, d), lambda i, e: (i, 0))],
        out_specs=row,
        out_shape=jax.ShapeDtypeStruct((t, d), F32),
        scratch_shapes=[pltpu.VMEM((d, tb), F32), pltpu.VMEM((eb, tb), F32),
                        pltpu.VMEM((eb, tb), BF16)],
        compiler_params=_cparams("arbitrary", "arbitrary"),
        name="peer_experts",
    )(h, r2, w2, lr, w1, u_bf16, vt_bf16, x, gate)


def _peer(x, gain, sc, sh, gate, w_query, sub_keys, expert_u, expert_v):
    h, s = _peer_scores(x, gain, sc, sh, jnp.transpose(w_query).astype(BF16),
                        sub_keys.astype(BF16))
    r2, w2, lr, w1 = _peer_topk(s)
    return _peer_expert(h, r2, w2, lr, w1, expert_u.astype(BF16),
                        jnp.transpose(expert_v).astype(BF16), x, gate)


def kernel(x_prompt, x_sample, cache_k_sb, cache_v_sb, cache_k_diff, cache_v_diff, page_table, c_prompt, c_sample, w_ada, b_ada, norm_mix, norm_ffn, w_qkv_sb, w_o_sb, w_qkv_diff, w_o_diff, diff_q_gain, diff_k_gain, diff_lambda_q1, diff_lambda_k1, diff_lambda_q2, diff_lambda_k2, diff_sub_gain, peer_w_query, peer_sub_keys, peer_u, peer_v):
    batch, seq, d = x_prompt.shape
    n_seq, n_new, _ = x_sample.shape
    assert d == SB_HEADS * HEAD_DIM == DIFF_HEADS * 2 * HEAD_DIM and n_new == SUBLANES
    depth = w_ada.shape[0]
    tp, ts = batch * seq, n_seq * n_new
    past_len = page_table.shape[1] * cache_k_sb.shape[2]

    x = jnp.concatenate([x_prompt.reshape(tp, d), x_sample.reshape(ts, d)], axis=0)
    c_all = jnp.concatenate([c_prompt, c_sample], axis=0)
    n_c = c_all.shape[0]
    c_all = jnp.pad(c_all, ((0, (-n_c) % SUBLANES), (0, 0)))
    mod = _ada(c_all, w_ada, b_ada)

    def per_group(layer, j):
        m = mod[layer, :, j * d:(j + 1) * d]
        return jnp.concatenate([jnp.repeat(m[:batch], seq // n_new, axis=0), m[batch:n_c]], axis=0)

    pos = jnp.concatenate([jnp.tile(jnp.arange(seq), batch),
                           jnp.tile(past_len + jnp.arange(n_new), n_seq)])
    cos_t, s1_t, s2_t = _rope_tables(pos)

    sb_k, sb_v, df_k, df_v = [], [], [], []
    for i in range(depth):
        sh_m, sc_m, g_m, sh_f, sc_f, g_f = [per_group(i, j) for j in range(6)]
        j = i // 2
        if i % 2 == 0:
            q, k, v = _qkv(x, norm_mix[i], sc_m, sh_m, w_qkv_sb[j].astype(BF16))
            o_p = _sb_prompt(q, k, v, batch, seq)
            o_s = _sb_decode(q, k, v, cache_k_sb, cache_v_sb, j, page_table, tp, HEAD_DIM ** -0.5)
            w_o = w_o_sb[j]
            sb_k.append(k)
            sb_v.append(v)
        else:
            lambda_init = 0.8 - 0.6 * math.exp(-0.3 * i)
            q, k, v = _qkv(x, norm_mix[i], sc_m, sh_m, w_qkv_diff[j].astype(BF16))
            q, k = _qknorm_rope(q, k, diff_q_gain[j], diff_k_gain[j], cos_t, s1_t, s2_t)
            lam_vecs = (diff_lambda_q1[j], diff_lambda_k1[j], diff_lambda_q2[j], diff_lambda_k2[j])
            o_p = _diff_prompt(q, k, v, lam_vecs, diff_sub_gain[j], lambda_init, batch, seq)
            o_s = _decode_attention(
                "diff", q, k, v, cache_k_diff, cache_v_diff, j, page_table, tp,
                extra=tuple(a.reshape(1, HEAD_DIM) for a in lam_vecs)
                + (diff_sub_gain[j].reshape(1, LANES),),
                scale=HEAD_DIM ** -0.5, lambda_init=lambda_init)
            w_o = w_o_diff[j]
            df_k.append(k)
            df_v.append(v)
        x = _outproj(jnp.concatenate([o_p, o_s], axis=0), w_o.astype(BF16), x, g_m)
        x = _peer(x, norm_ffn[i], sc_f, sh_f, g_f, peer_w_query[i], peer_sub_keys[i],
                  peer_u[i], peer_v[i])

    def rows(stack, lo, hi, shape):
        return jnp.stack([a[lo:hi].reshape(shape) for a in stack])

    t = tp + ts
    return (
        x[:tp].reshape(batch, seq, d),
        x[tp:].reshape(n_seq, n_new, d),
        rows(sb_k, 0, tp, (batch, seq, SB_HEADS, HEAD_DIM)),
        rows(sb_v, 0, tp, (batch, seq, SB_HEADS, HEAD_DIM)),
        rows(df_k, 0, tp, (batch, seq, DIFF_HEADS, 2, HEAD_DIM)),
        rows(df_v, 0, tp, (batch, seq, DIFF_HEADS, 2 * HEAD_DIM)),
        rows(sb_k, tp, t, (n_seq, n_new, SB_HEADS, HEAD_DIM)),
        rows(sb_v, tp, t, (n_seq, n_new, SB_HEADS, HEAD_DIM)),
        rows(df_k, tp, t, (n_seq, n_new, DIFF_HEADS, 2, HEAD_DIM)),
        rows(df_v, tp, t, (n_seq, n_new, DIFF_HEADS, 2 * HEAD_DIM)),
    )
```
